```python
import math
import jax
import jax.numpy as jnp
from jax import lax
import numpy as np

D_MODEL = 1024
BATCH = 32
SEQ = 256
DEPTH = 2
DEC_BATCH = 8
DEC_SEQ = 4096
PAST_LEN = 512

GRID_W = 64
ROPE_BASE = 10000.0
EPS = 1e-6
SSD_HEADS = 8
SSD_HEAD_DIM = 64
SSD_WIDTH = SSD_HEADS * SSD_HEAD_DIM
SSD_GROUPS = 2
SSD_STATE = 128
SSD_CHUNK = 128
CONV_W = 3
SSD_CONV_CH = SSD_WIDTH + 2 * SSD_GROUPS * SSD_STATE
MLA_HEADS = 8
MLA_Q_RANK = 256
MLA_KV_RANK = 128
MLA_NOPE = 64
MLA_ROPE_DIM = 32
MLA_V = 64
MLA_WIDTH = MLA_HEADS * MLA_V
MLA_SCALE = (MLA_NOPE + MLA_ROPE_DIM) ** -0.5
DIFF_HEADS = 4
DIFF_HEAD_DIM = 64
DIFF_WIDTH = DIFF_HEADS * 2 * DIFF_HEAD_DIM
DIFF_SCALE = DIFF_HEAD_DIM ** -0.5
MIX_WIDTH = SSD_WIDTH + MLA_WIDTH + DIFF_WIDTH
IN_WIDTHS = (SSD_WIDTH, SSD_CONV_CH, 2 * SSD_HEADS, MLA_Q_RANK, MLA_KV_RANK + MLA_ROPE_DIM,
             DIFF_WIDTH, DIFF_WIDTH, DIFF_WIDTH)
IN_TOTAL = sum(IN_WIDTHS)
IN_SPLITS = tuple(sum(IN_WIDTHS[:i + 1]) for i in range(len(IN_WIDTHS) - 1))
N_EXPERTS = 64
N_EXPERT_GROUPS = 8
TOPK_GROUPS = 4
TOP_K = 8
EXPERT_HIDDEN = 256
SHARED_HIDDEN = 256
ROUTED_SCALE = 2.5
EXPERT_BLOCK = 128
Q_BLOCK = 128
DENSE_KEY_LIMIT = 2048

kernel_name = 'hybrid_ssd_mla_diffattn_moe_diffusion_step'


def rms_norm(x, g):
    xf = x.astype(jnp.float32)
    y = xf * lax.rsqrt(jnp.mean(xf * xf, axis=-1, keepdims=True) + EPS)
    return (y * g.astype(jnp.float32)).astype(x.dtype)


def lambda_init(layer_idx):
    return 0.8 - 0.6 * math.exp(-0.3 * layer_idx)


def rope_1d(x, pos):
    half = x.shape[-1] // 2
    freq = ROPE_BASE ** (-jnp.arange(half, dtype=jnp.float32) / half)
    ang = pos.astype(jnp.float32)[:, None] * freq[None, :]
    shape = (1, x.shape[1]) + (1,) * (x.ndim - 3) + (half,)
    cos = jnp.cos(ang).reshape(shape)
    sin = jnp.sin(ang).reshape(shape)
    xf = x.astype(jnp.float32)
    x1, x2 = xf[..., :half], xf[..., half:]
    return jnp.concatenate([x1 * cos - x2 * sin, x2 * cos + x1 * sin], axis=-1).astype(x.dtype)


def rope_2d(x):
    n_tok = x.shape[1]
    t = jnp.arange(n_tok, dtype=jnp.int32)
    rows = t // GRID_W
    cols = t % GRID_W
    dr = x.shape[-1] // 2
    return jnp.concatenate([rope_1d(x[..., :dr], rows), rope_1d(x[..., dr:], cols)], axis=-1)


def dwconv_centred(x, w, b):
    ch = x.shape[-1]
    y = lax.conv_general_dilated(x, w[:, None, :].astype(x.dtype), window_strides=(1,), padding='SAME',
                                 dimension_numbers=('NWC', 'WIO', 'NWC'), feature_group_count=ch)
    return y + b


def segsum(x):
    T = x.shape[-1]
    xr = jnp.broadcast_to(x[..., :, None], x.shape + (T,))
    strict = jnp.tril(jnp.ones((T, T), dtype=bool), -1)
    xs = jnp.cumsum(jnp.where(strict, xr, 0.0), axis=-2)
    incl = jnp.tril(jnp.ones((T, T), dtype=bool))
    return jnp.where(incl, xs, -jnp.inf)


def ssd_scan(x, dt, a, b, c, init):
    n, L, H, P = x.shape
    nc = L // SSD_CHUNK
    rep = H // b.shape[2]
    f32 = jnp.float32
    bh = jnp.repeat(b.astype(f32), rep, axis=2).reshape(n, nc, SSD_CHUNK, H, -1)
    ch = jnp.repeat(c.astype(f32), rep, axis=2).reshape(n, nc, SSD_CHUNK, H, -1)
    xdt = (x.astype(f32) * dt[..., None]).reshape(n, nc, SSD_CHUNK, H, P)
    da = (dt * a).reshape(n, nc, SSD_CHUNK, H).transpose(0, 3, 1, 2)
    a_cum = jnp.cumsum(da, axis=-1)
    scores = jnp.einsum('bclhn,bcshn->bhcls', ch, bh) * jnp.exp(segsum(da))
    y_diag = jnp.einsum('bhcls,bcshp->bclhp', scores, xdt)
    decay_states = jnp.exp(a_cum[..., -1:] - a_cum)
    states = jnp.einsum('bclhn,bhcl,bclhp->bchpn', bh, decay_states, xdt)
    states = jnp.concatenate([init[:, None], states], axis=1)
    chunk_decay = jnp.exp(segsum(jnp.pad(a_cum[..., -1], ((0, 0), (0, 0), (1, 0)))))
    new_states = jnp.einsum('bhzc,bchpn->bzhpn', chunk_decay, states)
    states_in, final = new_states[:, :-1], new_states[:, -1]
    y_off = jnp.einsum('bclhn,bchpn,bhcl->bclhp', ch, states_in, jnp.exp(a_cum))
    y = (y_diag + y_off).reshape(n, L, H, P)
    return y.astype(x.dtype), final


def _rev(t):
    return jnp.flip(t, axis=1)


def over_query_blocks(fn, q, n_keys):
    if n_keys < DENSE_KEY_LIMIT:
        return fn(q)
    n, L = q.shape[:2]
    qb = jnp.moveaxis(q.reshape((n, L // Q_BLOCK, Q_BLOCK) + q.shape[2:]), 1, 0)
    out = jnp.moveaxis(lax.map(fn, qb), 0, 1)
    return out.reshape((n, L) + out.shape[3:])


def softmax_attention(q, k, v, scale):
    def attend(qb):
        s = jnp.einsum('bqhd,bkhd->bhqk', qb, k, preferred_element_type=jnp.float32) * scale
        p = jax.nn.softmax(s, axis=-1)
        return jnp.einsum('bhqk,bkhd->bqhd', p.astype(v.dtype), v)
    return over_query_blocks(attend, q, k.shape[1])


def differential_attention(q, k, v, lam, scale):
    def attend(qb):
        s = jnp.einsum('bqhmd,bkhmd->bhmqk', qb, k, preferred_element_type=jnp.float32) * scale
        p = jax.nn.softmax(s, axis=-1)
        w = p[:, :, 0] - lam * p[:, :, 1]
        return jnp.einsum('bhqk,bkhd->bqhd', w.astype(v.dtype), v)
    return over_query_blocks(attend, q, k.shape[1])


def swiglu(t, w_gate, w_up, w_down):
    return (jax.nn.silu(t @ w_gate) * (t @ w_up)) @ w_down


def routed_experts(t, idx, w, w_gate, w_up, w_down):
    T, D = t.shape
    A = T * TOP_K
    e_flat = idx.reshape(-1)
    tok_flat = jnp.arange(A, dtype=jnp.int32) // TOP_K
    w_flat = w.reshape(-1)
    order = jnp.argsort(e_flat)
    e_s, tok_s, w_s = e_flat[order], tok_flat[order], w_flat[order]
    counts = jnp.zeros((N_EXPERTS,), jnp.int32).at[e_flat].add(1)
    start = jnp.cumsum(counts) - counts
    padded = (counts + EXPERT_BLOCK - 1) // EXPERT_BLOCK * EXPERT_BLOCK
    pad_end = jnp.cumsum(padded)
    pad_start = pad_end - padded
    dest = pad_start[e_s] + (jnp.arange(A, dtype=jnp.int32) - start[e_s])
    n_blocks = -(-A // EXPERT_BLOCK) + N_EXPERTS
    slots = n_blocks * EXPERT_BLOCK
    slot_tok = jnp.full((slots,), T, jnp.int32).at[dest].set(tok_s)
    slot_w = jnp.zeros((slots,), jnp.float32).at[dest].set(w_s)
    block_start = jnp.arange(n_blocks, dtype=jnp.int32) * EXPERT_BLOCK
    block_exp = jnp.minimum(jnp.searchsorted(pad_end, block_start, side='right'), N_EXPERTS - 1)
    t_pad = jnp.concatenate([t, jnp.zeros((1, D), t.dtype)], axis=0)

    def expert_block(args):
        tok, wt, e = args
        xe = t_pad[tok]
        return swiglu(xe, w_gate[e], w_up[e], w_down[e]) * wt[:, None].astype(xe.dtype)

    yb = lax.map(expert_block, (slot_tok.reshape(n_blocks, EXPERT_BLOCK),
                                slot_w.reshape(n_blocks, EXPERT_BLOCK), block_exp))
    out = jnp.zeros((T + 1, D), t.dtype).at[slot_tok].add(yb.reshape(slots, D))
    return out[:T]


def moe_ffn(h, lp):
    n, L, D = h.shape
    t = h.reshape(n * L, D)
    scores = jax.nn.sigmoid((t @ lp['w_router']).astype(jnp.float32))
    sel = scores + lp['router_bias'].astype(jnp.float32)
    grp = sel.reshape(-1, N_EXPERT_GROUPS, N_EXPERTS // N_EXPERT_GROUPS)
    grp_score = jnp.sum(lax.top_k(grp, 2)[0], axis=-1)
    _, top_g = lax.top_k(grp_score, TOPK_GROUPS)
    gmask = jnp.any(top_g[:, :, None] == jnp.arange(N_EXPERT_GROUPS)[None, None, :], axis=1)
    sel = jnp.where(jnp.repeat(gmask, N_EXPERTS // N_EXPERT_GROUPS, axis=1), sel, -jnp.inf)
    _, idx = lax.top_k(sel, TOP_K)
    w = jnp.take_along_axis(scores, idx, axis=-1)
    w = w / jnp.sum(w, axis=-1, keepdims=True) * ROUTED_SCALE
    routed = routed_experts(t, idx, w, lp['w_e_gate'], lp['w_e_up'], lp['w_e_down'])
    shared = swiglu(t, lp['w_s_gate'], lp['w_s_up'], lp['w_s_down'])
    return (routed + shared).reshape(n, L, D)


def mixers(h, lp, layer_idx, ctx):
    n, L, _ = h.shape
    latent = ctx is not None
    f32 = jnp.float32
    u = h @ lp['w_in']
    z, xbc, dt_raw, q_a, kv_a, dq, dk, dv = jnp.split(u, IN_SPLITS, axis=-1)

    xbc = jax.nn.silu(dwconv_centred(xbc, lp['conv_w'], lp['conv_b']))
    xs, b_in, c_in = jnp.split(xbc, [SSD_WIDTH, SSD_WIDTH + SSD_GROUPS * SSD_STATE], axis=-1)
    x_h = xs.reshape(n, L, SSD_HEADS, SSD_HEAD_DIM)
    b_g = b_in.reshape(n, L, SSD_GROUPS, SSD_STATE)
    c_g = c_in.reshape(n, L, SSD_GROUPS, SSD_STATE)
    dt = jax.nn.softplus(dt_raw.astype(f32).reshape(n, L, 2, SSD_HEADS) + lp['dt_bias'].astype(f32))
    a = -jnp.exp(lp['a_log'].astype(f32))
    if latent:
        init = ctx['ssd'].astype(f32)
    else:
        init = jnp.zeros((n, 2, SSD_HEADS, SSD_HEAD_DIM, SSD_STATE), f32)
    y_f, s_f = ssd_scan(x_h, dt[:, :, 0], a[0], b_g, c_g, init[:, 0])
    y_b, s_b = ssd_scan(_rev(x_h), _rev(dt[:, :, 1]), a[1], _rev(b_g), _rev(c_g), init[:, 1])
    y = y_f + _rev(y_b) + x_h * lp['d_skip'][:, None]
    y_ssd = rms_norm(y.reshape(n, L, SSD_WIDTH) * jax.nn.silu(z), lp['ssd_norm_g'])

    q = (rms_norm(q_a, lp['q_a_norm_g']) @ lp['w_q_b']).reshape(n, L, MLA_HEADS, MLA_NOPE + MLA_ROPE_DIM)
    q_nope, q_rope = jnp.split(q, [MLA_NOPE], axis=-1)
    ckv = rms_norm(kv_a[..., :MLA_KV_RANK], lp['kv_a_norm_g'])
    k_rope = kv_a[..., MLA_KV_RANK:]
    if latent:
        q_rope = rope_2d(q_rope)
        k_rope = rope_2d(k_rope[:, :, None, :])[:, :, 0, :]
        ckv_all = jnp.concatenate([ckv, ctx['ckv']], axis=1)
        krope_all = jnp.concatenate([k_rope, ctx['krope']], axis=1)
    else:
        ckv_all, krope_all = ckv, k_rope
    n_keys = ckv_all.shape[1]
    kv = (ckv_all @ lp['w_kv_b']).reshape(n, n_keys, MLA_HEADS, MLA_NOPE + MLA_V)
    k_nope, v_mla = jnp.split(kv, [MLA_NOPE], axis=-1)
    k_mla = jnp.concatenate(
        [k_nope, jnp.broadcast_to(krope_all[:, :, None, :], (n, n_keys, MLA_HEADS, MLA_ROPE_DIM))], axis=-1)
    q_mla = jnp.concatenate([q_nope, q_rope], axis=-1)
    o_mla = softmax_attention(q_mla, k_mla, v_mla, MLA_SCALE).reshape(n, L, MLA_WIDTH)

    dq = dq.reshape(n, L, DIFF_HEADS, 2, DIFF_HEAD_DIM)
    dk = dk.reshape(n, L, DIFF_HEADS, 2, DIFF_HEAD_DIM)
    dv = dv.reshape(n, L, DIFF_HEADS, 2 * DIFF_HEAD_DIM)
    if latent:
        dq = rope_2d(dq)
        dk = rope_2d(dk)
        dk_all = jnp.concatenate([dk, ctx['dk'].reshape(n, -1, DIFF_HEADS, 2, DIFF_HEAD_DIM)], axis=1)
        dv_all = jnp.concatenate([dv, ctx['dv']], axis=1)
    else:
        dk_all, dv_all = dk, dv
    lam_init = lambda_init(layer_idx)
    lam = (jnp.exp(jnp.sum(lp['lambda_q1'].astype(f32) * lp['lambda_k1'].astype(f32)))
           - jnp.exp(jnp.sum(lp['lambda_q2'].astype(f32) * lp['lambda_k2'].astype(f32))) + lam_init)
    o = differential_attention(dq, dk_all, dv_all, lam, DIFF_SCALE)
    o_diff = (rms_norm(o, lp['diff_subln_g']) * (1.0 - lam_init)).reshape(n, L, DIFF_WIDTH)

    out = jnp.concatenate([y_ssd, o_mla, o_diff], axis=-1) @ lp['w_out']
    if latent:
        return out, None
    ctx_new = (jnp.stack([s_f, s_b], axis=1).astype(h.dtype), ckv, k_rope,
               dk.reshape(n, L, DIFF_HEADS, 2 * DIFF_HEAD_DIM), dv)
    return out, ctx_new


def block(x, cond, lp, layer_idx, ctx):
    mod = jax.nn.silu(cond) @ lp['w_ada'] + lp['b_ada']
    sh1, sc1, g1, sh2, sc2, g2 = jnp.split(mod[:, None, :], 6, axis=-1)
    h = rms_norm(x, lp['norm1_g']) * (1.0 + sc1) + sh1
    mix, ctx_new = mixers(h, lp, layer_idx, ctx)
    x = x + g1 * mix
    h = rms_norm(x, lp['norm2_g']) * (1.0 + sc2) + sh2
    x = x + g2 * moe_ffn(h, lp)
    return x, ctx_new


def setup_inputs(seed: int = 0) -> dict:
    key = jax.random.key(seed)
    ks = iter(jax.random.split(key, 64))

    def nrm(shape, scale):
        return jax.random.normal(next(ks), shape, jnp.float32) * scale

    def gain(shape):
        return 1.0 + nrm(shape, 0.01)

    D = D_MODEL
    dt0 = jnp.exp(jax.random.uniform(next(ks), (DEPTH, 2, SSD_HEADS), jnp.float32,
                                     minval=math.log(1e-3), maxval=math.log(1e-1)))
    dt_bias = dt0 + jnp.log(-jnp.expm1(-dt0))
    a_log = jnp.log(jax.random.uniform(next(ks), (DEPTH, 2, SSD_HEADS), jnp.float32, minval=1.0, maxval=16.0))
    return {
        'x_prompt': nrm((BATCH, SEQ, D), 1.0),
        'x_sample': nrm((DEC_BATCH, DEC_SEQ, D), 1.0),
        'state_ssd': nrm((DEC_BATCH, DEPTH, 2, SSD_HEADS, SSD_HEAD_DIM, SSD_STATE), 0.5),
        'cache_mla_ckv': nrm((DEC_BATCH, DEPTH, PAST_LEN, MLA_KV_RANK), 1.0),
        'cache_mla_krope': nrm((DEC_BATCH, DEPTH, PAST_LEN, MLA_ROPE_DIM), 1.0),
        'cache_diff_k': nrm((DEC_BATCH, DEPTH, PAST_LEN, DIFF_HEADS, 2 * DIFF_HEAD_DIM), 1.0),
        'cache_diff_v': nrm((DEC_BATCH, DEPTH, PAST_LEN, DIFF_HEADS, 2 * DIFF_HEAD_DIM), 1.0),
        'c': nrm((DEC_BATCH, D), 1.0),
        'c_ctx': nrm((D,), 1.0),
        'w_ada': nrm((DEPTH, D, 6 * D), 0.5 * D ** -0.5),
        'b_ada': nrm((DEPTH, 6 * D), 0.01),
        'norm1_g': gain((DEPTH, D)),
        'norm2_g': gain((DEPTH, D)),
        'w_in': nrm((DEPTH, D, IN_TOTAL), D ** -0.5),
        'conv_w': nrm((DEPTH, CONV_W, SSD_CONV_CH), CONV_W ** -0.5),
        'conv_b': nrm((DEPTH, SSD_CONV_CH), 0.01),
        'dt_bias': dt_bias,
        'a_log': a_log,
        'd_skip': 1.0 + nrm((DEPTH, SSD_HEADS), 0.1),
        'ssd_norm_g': gain((DEPTH, SSD_WIDTH)),
        'q_a_norm_g': gain((DEPTH, MLA_Q_RANK)),
        'w_q_b': nrm((DEPTH, MLA_Q_RANK, MLA_HEADS * (MLA_NOPE + MLA_ROPE_DIM)), MLA_Q_RANK ** -0.5),
        'kv_a_norm_g': gain((DEPTH, MLA_KV_RANK)),
        'w_kv_b': nrm((DEPTH, MLA_KV_RANK, MLA_HEADS * (MLA_NOPE + MLA_V)), MLA_KV_RANK ** -0.5),
        'lambda_q1': nrm((DEPTH, DIFF_HEAD_DIM), 0.1),
        'lambda_k1': nrm((DEPTH, DIFF_HEAD_DIM), 0.1),
        'lambda_q2': nrm((DEPTH, DIFF_HEAD_DIM), 0.1),
        'lambda_k2': nrm((DEPTH, DIFF_HEAD_DIM), 0.1),
        'diff_subln_g': gain((DEPTH, 2 * DIFF_HEAD_DIM)),
        'w_out': nrm((DEPTH, MIX_WIDTH, D), MIX_WIDTH ** -0.5),
        'w_router': nrm((DEPTH, D, N_EXPERTS), D ** -0.5),
        'router_bias': nrm((DEPTH, N_EXPERTS), 0.01),
        'w_e_gate': nrm((DEPTH, N_EXPERTS, D, EXPERT_HIDDEN), D ** -0.5),
        'w_e_up': nrm((DEPTH, N_EXPERTS, D, EXPERT_HIDDEN), D ** -0.5),
        'w_e_down': nrm((DEPTH, N_EXPERTS, EXPERT_HIDDEN, D), EXPERT_HIDDEN ** -0.5),
        'w_s_gate': nrm((DEPTH, D, SHARED_HIDDEN), D ** -0.5),
        'w_s_up': nrm((DEPTH, D, SHARED_HIDDEN), D ** -0.5),
        'w_s_down': nrm((DEPTH, SHARED_HIDDEN, D), SHARED_HIDDEN ** -0.5),
        'final_norm_g': gain((D,)),
    }


def reference(x_prompt, x_sample, state_ssd, cache_mla_ckv, cache_mla_krope, cache_diff_k, cache_diff_v,
              c, c_ctx, w_ada, b_ada, norm1_g, norm2_g, w_in, conv_w, conv_b, dt_bias, a_log, d_skip,
              ssd_norm_g, q_a_norm_g, w_q_b, kv_a_norm_g, w_kv_b, lambda_q1, lambda_k1, lambda_q2, lambda_k2,
              diff_subln_g, w_out, w_router, router_bias, w_e_gate, w_e_up, w_e_down, w_s_gate, w_s_up,
              w_s_down, final_norm_g):
    xp = x_prompt
    xs = x_sample
    ssd_l, ckv_l, krope_l, dk_l, dv_l = [], [], [], [], []
    for l in range(DEPTH):
        lp = {
            'w_ada': w_ada[l], 'b_ada': b_ada[l], 'norm1_g': norm1_g[l], 'norm2_g': norm2_g[l],
            'w_in': w_in[l], 'conv_w': conv_w[l], 'conv_b': conv_b[l], 'dt_bias': dt_bias[l],
            'a_log': a_log[l], 'd_skip': d_skip[l], 'ssd_norm_g': ssd_norm_g[l],
            'q_a_norm_g': q_a_norm_g[l], 'w_q_b': w_q_b[l], 'kv_a_norm_g': kv_a_norm_g[l],
            'w_kv_b': w_kv_b[l], 'lambda_q1': lambda_q1[l], 'lambda_k1': lambda_k1[l],
            'lambda_q2': lambda_q2[l], 'lambda_k2': lambda_k2[l], 'diff_subln_g': diff_subln_g[l],
            'w_out': w_out[l], 'w_router': w_router[l], 'router_bias': router_bias[l],
            'w_e_gate': w_e_gate[l], 'w_e_up': w_e_up[l], 'w_e_down': w_e_down[l],
            'w_s_gate': w_s_gate[l], 'w_s_up': w_s_up[l], 'w_s_down': w_s_down[l],
        }
        xp, ctx_new = block(xp, c_ctx[None, :], lp, l, None)
        ssd_l.append(ctx_new[0])
        ckv_l.append(ctx_new[1])
        krope_l.append(ctx_new[2])
        dk_l.append(ctx_new[3])
        dv_l.append(ctx_new[4])
        cached = {'ssd': state_ssd[:, l], 'ckv': cache_mla_ckv[:, l], 'krope': cache_mla_krope[:, l],
                  'dk': cache_diff_k[:, l], 'dv': cache_diff_v[:, l]}
        xs, _ = block(xs, c, lp, l, cached)
    y_prompt = rms_norm(xp, final_norm_g)
    y_sample = rms_norm(xs, final_norm_g)
    new_state_ssd = jnp.stack(ssd_l, axis=1)
    new_cache_mla_ckv = jnp.stack(ckv_l, axis=1)
    new_cache_mla_krope = jnp.stack(krope_l, axis=1)
    new_cache_diff_k = jnp.stack(dk_l, axis=1)
    new_cache_diff_v = jnp.stack(dv_l, axis=1)
    return (y_prompt, y_sample, new_state_ssd, new_cache_mla_ckv, new_cache_mla_krope, new_cache_diff_k, new_cache_diff_v)
```

```python
import functools
import math

import jax
import jax.numpy as jnp
from jax import lax
from jax.experimental import pallas as pl
from jax.experimental.pallas import tpu as pltpu

F32 = jnp.float32
BF16 = jnp.bfloat16

GRID_W = 64
ROPE_BASE = 10000.0
EPS = 1e-6
SSD_HEADS = 8
SSD_HEAD_DIM = 64
SSD_WIDTH = 512
SSD_GROUPS = 2
SSD_STATE = 128
SSD_CHUNK = 128
SSD_CONV_CH = 1024
MLA_HEADS = 8
MLA_Q_RANK = 256
MLA_KV_RANK = 128
MLA_NOPE = 64
MLA_ROPE_DIM = 32
MLA_V = 64
MLA_SCALE = (MLA_NOPE + MLA_ROPE_DIM) ** -0.5
DIFF_HEADS = 4
DIFF_HEAD_DIM = 64
DIFF_WIDTH = 512
DIFF_SCALE = DIFF_HEAD_DIM ** -0.5
N_EXPERTS = 64
N_EXPERT_GROUPS = 8
TOPK_GROUPS = 4
TOP_K = 8
EXPERT_HIDDEN = 256
ROUTED_SCALE = 2.5

LANE = 128
VMEM_LIMIT = 56 * 1024 * 1024
EXPERT_ROWS = 256
NEG_INF = float("-inf")


def _cparams(sem):
    return pltpu.CompilerParams(dimension_semantics=sem, vmem_limit_bytes=VMEM_LIMIT)


def _silu(x):
    return x * (1.0 / (1.0 + jnp.exp(-x)))


def _softplus(x):
    return jnp.maximum(x, 0.0) + jnp.log1p(jnp.exp(-jnp.abs(x)))


def _rms(x, g):
    return x * lax.rsqrt(jnp.mean(x * x, axis=-1, keepdims=True) + EPS) * g


def _dot(a, b):
    return jnp.dot(a, b, preferred_element_type=F32)


def _dot_nt(a, b):
    return lax.dot_general(a, b, (((1,), (1,)), ((), ())), preferred_element_type=F32)


def _lane_iota(shape):
    return lax.broadcasted_iota(jnp.int32, shape, len(shape) - 1)


def _rot_half(x, half):
    lane = _lane_iota(x.shape)
    up = pltpu.roll(x, LANE - half, axis=1)
    dn = pltpu.roll(x, half, axis=1)
    return jnp.where((lane & (2 * half - 1)) < half, up, dn)


def _ada_kernel(c_ref, w_ref, b_ref, o_ref):
    s = _silu(c_ref[...])
    o_ref[...] = _dot(s.astype(BF16), w_ref[...].astype(BF16)) + b_ref[...]


def _ada(cond, w_ada, b_ada):
    depth, d, n6 = w_ada.shape
    rows = cond.shape[0]
    tn = 1536
    return pl.pallas_call(
        _ada_kernel,
        grid=(depth, n6 // tn),
        in_specs=[pl.BlockSpec((rows, d), lambda l, j: (0, 0)),
                  pl.BlockSpec((None, d, tn), lambda l, j: (l, 0, j)),
                  pl.BlockSpec((None, 1, tn), lambda l, j: (l, 0, j))],
        out_specs=pl.BlockSpec((None, rows, tn), lambda l, j: (l, 0, j)),
        out_shape=jax.ShapeDtypeStruct((depth, rows, n6), F32),
        compiler_params=_cparams(("arbitrary", "arbitrary")),
        name="ada",
    )(cond, w_ada, b_ada.reshape(depth, 1, n6))


_C_Z, _C_XBC, _C_MISC, _C_QA, _C_CKV, _C_DQ, _C_DK, _C_DV, _C_END = (
    0, 512, 1536, 1664, 1920, 2048, 2560, 3072, 3584)
_KROPE_LANE = 64


def _inproj_kernel(latent, x_ref, sc_ref, sh_ref, g1_ref, w_ref, gqa_ref, gkv_ref, wqb_ref, wk_ref, wv_ref,
                   *rest):
    if latent:
        (cq_ref, sq_ref, cd_ref, sd_ref,
         z_ref, xbc_ref, misc_ref, q_ref, k_ref, v_ref, dq_ref, dk_ref, dv_ref) = rest
    else:
        (z_ref, xbc_ref, misc_ref, q_ref, k_ref, v_ref, dq_ref, dk_ref, dv_ref,
         ckv_ref, dkf_ref, dvf_ref) = rest
    x = x_ref[...]
    h = (_rms(x, g1_ref[...]) * (1.0 + sc_ref[...]) + sh_ref[...]).astype(BF16)

    def seg(a, b):
        return _dot(h, w_ref[:, a:b])

    z_ref[...] = seg(_C_Z, _C_XBC).astype(BF16)
    xbc_ref[...] = seg(_C_XBC, _C_MISC).astype(BF16)
    misc = seg(_C_MISC, _C_QA)
    misc_ref[...] = misc

    qn = _rms(seg(_C_QA, _C_CKV), gqa_ref[...]).astype(BF16)
    q = _dot(qn, wqb_ref[...]) * MLA_SCALE
    ckv = _rms(seg(_C_CKV, _C_DQ), gkv_ref[...])
    ckv_b = ckv.astype(BF16)
    kn = _dot(ckv_b, wk_ref[...])
    v_ref[...] = _dot(ckv_b, wv_ref[...]).astype(BF16)
    lane = _lane_iota(misc.shape)
    in_rope = (lane >= _KROPE_LANE) & (lane < _KROPE_LANE + MLA_ROPE_DIM)
    if latent:
        cq, sq = cq_ref[...], sq_ref[...]
        kr = jnp.where(in_rope, misc * cq + _rot_half(misc, MLA_ROPE_DIM // 4) * sq, 0.0)
    else:
        kr = jnp.where(in_rope, misc, 0.0)
    for hd in range(MLA_HEADS):
        sl = slice(hd * LANE, (hd + 1) * LANE)
        qh = q[:, sl]
        if latent:
            qh = qh * cq + _rot_half(qh, MLA_ROPE_DIM // 4) * sq
        q_ref[:, sl] = qh.astype(BF16)
        k_ref[:, sl] = (kn[:, sl] + kr).astype(BF16)

    dq = seg(_C_DQ, _C_DK) * DIFF_SCALE
    dk = seg(_C_DK, _C_DV)
    dv = seg(_C_DV, _C_END)
    dv_ref[...] = dv.astype(BF16)
    if latent:
        cd, sd = cd_ref[...], sd_ref[...]
        for hd in range(DIFF_HEADS):
            sl = slice(hd * LANE, (hd + 1) * LANE)
            a = dq[:, sl]
            b = dk[:, sl]
            dq_ref[:, sl] = (a * cd + _rot_half(a, DIFF_HEAD_DIM // 4) * sd).astype(BF16)
            dk_ref[:, sl] = (b * cd + _rot_half(b, DIFF_HEAD_DIM // 4) * sd).astype(BF16)
    else:
        dq_ref[...] = dq.astype(BF16)
        dk_ref[...] = dk.astype(BF16)
        ckv_ref[...] = ckv
        dkf_ref[...] = dk
        dvf_ref[...] = dv


def _inproj(x2d, sc, sh, g1, lw, seq_len, tables):
    t_tot, d = x2d.shape
    latent = tables is not None
    tm = min(512, seq_len)
    tpb = seq_len // tm
    nb = sc.shape[0]
    bidx = (lambda i: (i // tpb, 0, 0)) if nb > 1 else (lambda i: (0, 0, 0))
    row = lambda i: (i, 0)
    const = lambda i: (0, 0)
    in_specs = [pl.BlockSpec((tm, d), row),
                pl.BlockSpec((None, 1, d), bidx), pl.BlockSpec((None, 1, d), bidx),
                pl.BlockSpec((1, d), const),
                pl.BlockSpec((d, _C_END), const),
                pl.BlockSpec((1, MLA_Q_RANK), const), pl.BlockSpec((1, MLA_KV_RANK), const),
                pl.BlockSpec((MLA_Q_RANK, MLA_HEADS * LANE), const),
                pl.BlockSpec((MLA_KV_RANK, MLA_HEADS * LANE), const),
                pl.BlockSpec((MLA_KV_RANK, MLA_HEADS * LANE), const)]
    args = [x2d, sc, sh, g1, lw["w_in"], lw["gqa"], lw["gkv"], lw["w_qb"], lw["w_k"], lw["w_v"]]
    if latent:
        tab = lambda i: (i % tpb, 0)
        in_specs += [pl.BlockSpec((tm, LANE), tab)] * 4
        args += list(tables)

    def o(width, dtype):
        return pl.BlockSpec((tm, width), row), jax.ShapeDtypeStruct((t_tot, width), dtype)

    outs = [o(512, BF16), o(1024, BF16), o(LANE, F32), o(1024, BF16), o(1024, BF16), o(1024, BF16),
            o(512, BF16), o(512, BF16), o(512, BF16)]
    if not latent:
        outs += [o(MLA_KV_RANK, F32), o(512, F32), o(512, F32)]
    return pl.pallas_call(
        functools.partial(_inproj_kernel, latent),
        grid=(t_tot // tm,),
        in_specs=in_specs,
        out_specs=[s for s, _ in outs],
        out_shape=[s for _, s in outs],
        compiler_params=_cparams(("arbitrary",)),
        name="inproj_lat" if latent else "inproj_ctx",
    )(*args)


def _ctxkv_kernel(ckv_ref, kr_ref, wk_ref, wv_ref, k_ref, v_ref):
    c = ckv_ref[...].astype(BF16)
    kn = _dot(c, wk_ref[...])
    kr = kr_ref[...]
    for hd in range(MLA_HEADS):
        sl = slice(hd * LANE, (hd + 1) * LANE)
        k_ref[:, sl] = (kn[:, sl] + kr).astype(BF16)
    v_ref[...] = _dot(c, wv_ref[...]).astype(BF16)


def _ctxkv(ckv2d, kr_pad, lw):
    t_tot = ckv2d.shape[0]
    tm = min(512, t_tot)
    row = lambda i: (i, 0)
    const = lambda i: (0, 0)
    w = MLA_HEADS * LANE
    return pl.pallas_call(
        _ctxkv_kernel,
        grid=(t_tot // tm,),
        in_specs=[pl.BlockSpec((tm, MLA_KV_RANK), row), pl.BlockSpec((tm, LANE), row),
                  pl.BlockSpec((MLA_KV_RANK, w), const), pl.BlockSpec((MLA_KV_RANK, w), const)],
        out_specs=[pl.BlockSpec((tm, w), row), pl.BlockSpec((tm, w), row)],
        out_shape=[jax.ShapeDtypeStruct((t_tot, w), BF16)] * 2,
        compiler_params=_cparams(("arbitrary",)),
        name="ctxkv",
    )(ckv2d, kr_pad, lw["w_k"], lw["w_v"])


_HALO = 16


def _ssd_kernel(nblk, tb, has_init, want_state, *refs):
    it = iter(refs)
    xbc_ref, prev_ref, next_ref, misc_ref, z_ref = [next(it) for _ in range(5)]
    cw_ref, cb_ref, dtb_ref, alog_ref, dsk_ref, gn_ref = [next(it) for _ in range(6)]
    init_ref = next(it) if has_init else None
    y_ref = next(it)
    st_ref = next(it) if want_state else None
    yf_s, s_s, xc_s, dt_s, da_s = [next(it) for _ in range(5)]

    j = pl.program_id(1)
    nch = tb // SSD_CHUNK
    blk = jnp.where(j < nblk, j, 2 * nblk - 1 - j)

    xb = xbc_ref[...].astype(F32)
    rowid = lax.broadcasted_iota(jnp.int32, xb.shape, 0)
    prev_row = prev_ref[_HALO - 1:_HALO, :].astype(F32) * (blk > 0).astype(F32)
    next_row = next_ref[0:1, :].astype(F32) * (blk < nblk - 1).astype(F32)
    xm1 = jnp.where(rowid == 0, prev_row, pltpu.roll(xb, 1, axis=0))
    xp1 = jnp.where(rowid == tb - 1, next_row, pltpu.roll(xb, tb - 1, axis=0))
    cw = cw_ref[...]
    conv = xm1 * cw[0:1, :] + xb * cw[1:2, :] + xp1 * cw[2:3, :] + cb_ref[...]
    xc_s[...] = _silu(conv)
    dt = _softplus(misc_ref[...] + dtb_ref[...])
    dt_s[...] = dt
    da_s[...] = dt * (-jnp.exp(alog_ref[...]))

    def load_init(d):
        if has_init:
            s_s[...] = init_ref[d]
        else:
            s_s[...] = jnp.zeros(s_s.shape, F32)

    @pl.when(j == 0)
    def _():
        load_init(0)

    @pl.when(j == nblk)
    def _():
        if want_state:
            st_ref[0] = s_s[...]
        load_init(1)

    ri = lax.broadcasted_iota(jnp.int32, (SSD_CHUNK, SSD_CHUNK), 0)
    ci = lax.broadcasted_iota(jnp.int32, (SSD_CHUNK, SSD_CHUNK), 1)
    lane = ci
    lo = lane < SSD_HEAD_DIM

    def bcol(m, li):
        return jnp.broadcast_to(m[:, li:li + 1], (SSD_CHUNK, SSD_CHUNK))

    def pair_cols(m, li):
        return jnp.where(lo, bcol(m, li), bcol(m, li + 1))

    def chunk_step(c, d):
        r0 = pl.multiple_of(c * SSD_CHUNK, SSD_CHUNK)
        rows = pl.ds(r0, SSD_CHUNK)
        mask = (ri >= ci) if d == 0 else (ri <= ci)
        da_c = da_s[rows, :]
        acum = jnp.dot(mask.astype(F32), da_c, preferred_element_type=F32, precision=lax.Precision.HIGHEST)
        acum_t = acum.T
        tot = acum[SSD_CHUNK - 1:SSD_CHUNK, :] if d == 0 else acum[0:1, :]
        eac = jnp.exp(acum)
        wdec = jnp.exp(tot - acum)
        etot = jnp.exp(tot)
        dt_c = dt_s[rows, :]
        xs = xc_s[rows, 0:SSD_WIDTH]
        y_parts = []
        for g in range(SSD_GROUPS):
            b_g = xc_s[rows, SSD_WIDTH + g * SSD_STATE:SSD_WIDTH + (g + 1) * SSD_STATE].astype(BF16)
            c0 = SSD_WIDTH + SSD_GROUPS * SSD_STATE + g * SSD_STATE
            c_g = xc_s[rows, c0:c0 + SSD_STATE].astype(BF16)
            cb = _dot_nt(c_g, b_g)
            s_g = s_s[g]
            yoff = _dot_nt(c_g, s_g.astype(BF16))
            xw_parts = []
            for pp in range(2):
                h0 = 4 * g + 2 * pp
                li = SSD_HEADS * d + h0
                sc = []
                for e in range(2):
                    col = bcol(acum, li + e)
                    rowb = jnp.broadcast_to(acum_t[li + e:li + e + 1, :], (SSD_CHUNK, SSD_CHUNK))
                    lm = jnp.exp(jnp.where(mask, col - rowb, NEG_INF))
                    sc.append((cb * lm).astype(BF16))
                scp = jnp.concatenate(sc, axis=1)
                xdt = xs[:, LANE * (2 * g + pp):LANE * (2 * g + pp + 1)] * pair_cols(dt_c, li)
                bd = jnp.concatenate([jnp.where(lo, xdt, 0.0), jnp.where(lo, 0.0, xdt)], axis=0).astype(BF16)
                ydiag = _dot(scp, bd)
                y_parts.append(ydiag + yoff[:, LANE * pp:LANE * (pp + 1)] * pair_cols(eac, li))
                xw_parts.append(xdt * pair_cols(wdec, li))
            xw_t = jnp.concatenate(xw_parts, axis=1).T.astype(BF16)
            ds = _dot(xw_t, b_g)
            dec = jnp.concatenate(
                [jnp.broadcast_to(etot[:, SSD_HEADS * d + 4 * g + hh:SSD_HEADS * d + 4 * g + hh + 1],
                                  (SSD_HEAD_DIM, SSD_STATE)) for hh in range(4)], axis=0)
            s_s[g] = s_g * dec + ds
        y_c = jnp.concatenate(y_parts, axis=1)
        arow = pl.ds(pl.multiple_of(blk * tb + r0, SSD_CHUNK), SSD_CHUNK)
        if d == 0:
            yf_s[arow, :] = y_c + xs * dsk_ref[...]
        else:
            y = yf_s[arow, :] + y_c
            gated = y * _silu(z_ref[rows, :].astype(F32))
            y_ref[rows, :] = _rms(gated, gn_ref[...]).astype(BF16)

    @pl.when(j < nblk)
    def _():
        def body(c, carry):
            chunk_step(c, 0)
            return carry
        lax.fori_loop(0, nch, body, 0)

    @pl.when(j >= nblk)
    def _():
        def body(c, carry):
            chunk_step(nch - 1 - c, 1)
            return carry
        lax.fori_loop(0, nch, body, 0)

    if want_state:
        @pl.when(j == 2 * nblk - 1)
        def _():
            st_ref[1] = s_s[...]


def _ssd(xbc, misc, z, lw, n, seq_len, init, want_state):
    tb = min(512, seq_len)
    nblk = seq_len // tb
    hpb = tb // _HALO
    has_init = init is not None

    def blk_of(j):
        return jnp.where(j < nblk, j, 2 * nblk - 1 - j)

    main = lambda b, j: (b * nblk + blk_of(j), 0)
    prev = lambda b, j: (jnp.maximum((b * nblk + blk_of(j)) * hpb - 1, 0), 0)
    nxt = lambda b, j: (jnp.minimum((b * nblk + blk_of(j) + 1) * hpb, n * nblk * hpb - 1), 0)
    const = lambda b, j: (0, 0)
    in_specs = [pl.BlockSpec((tb, SSD_CONV_CH), main),
                pl.BlockSpec((_HALO, SSD_CONV_CH), prev),
                pl.BlockSpec((_HALO, SSD_CONV_CH), nxt),
                pl.BlockSpec((tb, LANE), main),
                pl.BlockSpec((tb, SSD_WIDTH), main),
                pl.BlockSpec((3, SSD_CONV_CH), const), pl.BlockSpec((1, SSD_CONV_CH), const),
                pl.BlockSpec((1, LANE), const), pl.BlockSpec((1, LANE), const),
                pl.BlockSpec((1, SSD_WIDTH), const), pl.BlockSpec((1, SSD_WIDTH), const)]
    args = [xbc, xbc, xbc, misc, z, lw["conv_w"], lw["conv_b"], lw["dt_bias"], lw["a_log"], lw["d_skip"],
            lw["ssd_norm_g"]]
    st_block = (None, 2, SSD_GROUPS, 4 * SSD_HEAD_DIM, SSD_STATE)
    if has_init:
        in_specs.append(pl.BlockSpec(st_block, lambda b, j: (b, 0, 0, 0, 0)))
        args.append(init)
    omap = lambda b, j: (b * nblk + jnp.where(j < nblk, nblk - 1, 2 * nblk - 1 - j), 0)
    out_specs = [pl.BlockSpec((tb, SSD_WIDTH), omap)]
    out_shape = [jax.ShapeDtypeStruct((n * seq_len, SSD_WIDTH), BF16)]
    if want_state:
        out_specs.append(pl.BlockSpec(st_block, lambda b, j: (b, 0, 0, 0, 0)))
        out_shape.append(jax.ShapeDtypeStruct((n,) + st_block[1:], F32))
    res = pl.pallas_call(
        functools.partial(_ssd_kernel, nblk, tb, has_init, want_state),
        grid=(n, 2 * nblk),
        in_specs=in_specs,
        out_specs=out_specs,
        out_shape=out_shape,
        scratch_shapes=[pltpu.VMEM((seq_len, SSD_WIDTH), F32),
                        pltpu.VMEM((SSD_GROUPS, 4 * SSD_HEAD_DIM, SSD_STATE), F32),
                        pltpu.VMEM((tb, SSD_CONV_CH), F32),
                        pltpu.VMEM((tb, LANE), F32),
                        pltpu.VMEM((tb, LANE), F32)],
        compiler_params=_cparams(("arbitrary", "arbitrary")),
        name="ssd_lat" if has_init else "ssd_ctx",
    )(*args)
    return res if want_state else (res[0], None)


def _online_softmax_step(s, m_ref, l_ref, idx):
    m_prev = m_ref[idx]
    m_new = jnp.maximum(m_prev, jnp.max(s, axis=-1, keepdims=True))
    alpha = jnp.exp(m_prev - m_new)
    p = jnp.exp(s - m_new[:, 0:1])
    l_ref[idx] = alpha * l_ref[idx] + jnp.sum(p, axis=-1, keepdims=True)
    m_ref[idx] = m_new
    return alpha, p.astype(BF16)


def _mla_kernel(n_lat, has_ctx, *refs):
    if has_ctx:
        q_ref, k_ref, v_ref, kc_ref, vc_ref, o_ref, m_ref, l_ref, acc_ref = refs
    else:
        q_ref, k_ref, v_ref, o_ref, m_ref, l_ref, acc_ref = refs
    ki = pl.program_id(2)
    nk = pl.num_programs(2)

    @pl.when(ki == 0)
    def _():
        m_ref[...] = jnp.full(m_ref.shape, NEG_INF, F32)
        l_ref[...] = jnp.zeros(l_ref.shape, F32)
        acc_ref[...] = jnp.zeros(acc_ref.shape, F32)

    lo = _lane_iota((q_ref.shape[0], LANE)) < MLA_V

    def process(kr, vr):
        for jp in range(MLA_HEADS // 2):
            alphas, pvs = [], []
            for e in range(2):
                hd = 2 * jp + e
                sl = slice(hd * LANE, (hd + 1) * LANE)
                s = _dot_nt(q_ref[:, sl], kr[:, sl])
                alpha, p = _online_softmax_step(s, m_ref, l_ref, hd)
                alphas.append(alpha)
                pvs.append(_dot(p, vr[:, sl]))
            acc_ref[jp] = acc_ref[jp] * jnp.where(lo, alphas[0], alphas[1]) + pvs[0] + pvs[1]

    if has_ctx:
        @pl.when(ki < n_lat)
        def _():
            process(k_ref, v_ref)

        @pl.when(ki >= n_lat)
        def _():
            process(kc_ref, vc_ref)
    else:
        process(k_ref, v_ref)

    @pl.when(ki == nk - 1)
    def _():
        for jp in range(MLA_HEADS // 2):
            den = jnp.where(lo, l_ref[2 * jp], l_ref[2 * jp + 1])
            o_ref[:, jp * LANE:(jp + 1) * LANE] = (acc_ref[jp] / den).astype(BF16)


def _diff_kernel(n_lat, has_ctx, lam_init, *refs):
    if has_ctx:
        q_ref, k_ref, v_ref, kc_ref, vc_ref, lamv_ref, g_ref, o_ref, m_ref, l_ref, acc_ref = refs
    else:
        q_ref, k_ref, v_ref, lamv_ref, g_ref, o_ref, m_ref, l_ref, acc_ref = refs
    ki = pl.program_id(2)
    nk = pl.num_programs(2)

    @pl.when(ki == 0)
    def _():
        m_ref[...] = jnp.full(m_ref.shape, NEG_INF, F32)
        l_ref[...] = jnp.zeros(l_ref.shape, F32)
        acc_ref[...] = jnp.zeros(acc_ref.shape, F32)

    lo = _lane_iota((q_ref.shape[0], LANE)) < DIFF_HEAD_DIM

    def process(kr, vr):
        for hd in range(DIFF_HEADS):
            sl = slice(hd * LANE, (hd + 1) * LANE)
            qh = q_ref[:, sl]
            kh = kr[:, sl]
            vh = vr[:, sl]
            zero = jnp.zeros_like(qh)
            for mp in range(2):
                qm = jnp.where(lo, qh, zero) if mp == 0 else jnp.where(lo, zero, qh)
                s = _dot_nt(qm, kh)
                idx = 2 * hd + mp
                alpha, p = _online_softmax_step(s, m_ref, l_ref, idx)
                acc_ref[idx] = acc_ref[idx] * alpha + _dot(p, vh)

    if has_ctx:
        @pl.when(ki < n_lat)
        def _():
            process(k_ref, v_ref)

        @pl.when(ki >= n_lat)
        def _():
            process(kc_ref, vc_ref)
    else:
        process(k_ref, v_ref)

    @pl.when(ki == nk - 1)
    def _():
        lv = lamv_ref[...]
        lam = (jnp.exp(jnp.sum(lv[0:1, :] * lv[1:2, :], axis=-1, keepdims=True))
               - jnp.exp(jnp.sum(lv[2:3, :] * lv[3:4, :], axis=-1, keepdims=True)) + lam_init)
        for hd in range(DIFF_HEADS):
            o = acc_ref[2 * hd] / l_ref[2 * hd] - lam * (acc_ref[2 * hd + 1] / l_ref[2 * hd + 1])
            o_ref[:, hd * LANE:(hd + 1) * LANE] = (_rms(o, g_ref[...]) * (1.0 - lam_init)).astype(BF16)


def _attention(kind, q, k, v, kc, vc, n, seq_len, past_len, extra, lam_init=None):
    has_ctx = kc is not None
    tq = min(512, seq_len)
    tk = min(512, seq_len)
    nq = seq_len // tq
    n_lat = seq_len // tk
    n_ctx = (past_len // tk) if has_ctx else 0
    assert not has_ctx or past_len % tk == 0
    wq, wk, wv = q.shape[1], k.shape[1], v.shape[1]
    qmap = lambda b, qi, ki: (b * nq + qi, 0)
    kmap = lambda b, qi, ki: (b * n_lat + jnp.minimum(ki, n_lat - 1), 0)
    cmap = lambda b, qi, ki: (b * max(n_ctx, 1) + jnp.maximum(ki - n_lat, 0), 0)
    in_specs = [pl.BlockSpec((tq, wq), qmap), pl.BlockSpec((tk, wk), kmap), pl.BlockSpec((tk, wv), kmap)]
    args = [q, k, v]
    if has_ctx:
        in_specs += [pl.BlockSpec((tk, wk), cmap), pl.BlockSpec((tk, wv), cmap)]
        args += [kc, vc]
    if kind == "mla":
        body = functools.partial(_mla_kernel, n_lat, has_ctx)
        n_stat, n_acc, w_out = MLA_HEADS, MLA_HEADS // 2, MLA_HEADS * MLA_V
    else:
        body = functools.partial(_diff_kernel, n_lat, has_ctx, lam_init)
        n_stat, n_acc, w_out = 2 * DIFF_HEADS, 2 * DIFF_HEADS, DIFF_WIDTH
        lamv, g = extra
        in_specs += [pl.BlockSpec(lamv.shape, lambda b, qi, ki: (0, 0)),
                     pl.BlockSpec(g.shape, lambda b, qi, ki: (0, 0))]
        args += [lamv, g]
    return pl.pallas_call(
        body,
        grid=(n, nq, n_lat + n_ctx),
        in_specs=in_specs,
        out_specs=pl.BlockSpec((tq, w_out), qmap),
        out_shape=jax.ShapeDtypeStruct((n * seq_len, w_out), BF16),
        scratch_shapes=[pltpu.VMEM((n_stat, tq, LANE), F32), pltpu.VMEM((n_stat, tq, LANE), F32),
                        pltpu.VMEM((n_acc, tq, LANE), F32)],
        compiler_params=_cparams(("arbitrary", "arbitrary", "arbitrary")),
        name=kind + ("_lat" if has_ctx else "_ctx"),
    )(*args)


def _outproj_kernel(ys_ref, om_ref, od_ref, x_ref, w_ref, g1_ref, n2_ref, sc_ref, sh_ref, wr_ref,
                    x1_ref, h2_ref, sco_ref):
    mix = (_dot(ys_ref[...], w_ref[0:512, :]) + _dot(om_ref[...], w_ref[512:1024, :])
           + _dot(od_ref[...], w_ref[1024:1536, :]))
    x1 = x_ref[...] + g1_ref[...] * mix
    x1_ref[...] = x1
    h2 = (_rms(x1, n2_ref[...]) * (1.0 + sc_ref[...]) + sh_ref[...]).astype(BF16)
    h2_ref[...] = h2
    sco_ref[...] = 1.0 / (1.0 + jnp.exp(-_dot(h2, wr_ref[...])))


def _outproj(ys, om, od, x2d, g1, sc2, sh2, lw, seq_len):
    t_tot, d = x2d.shape
    tm = min(512, seq_len)
    tpb = seq_len // tm
    nb = g1.shape[0]
    bidx = (lambda i: (i // tpb, 0, 0)) if nb > 1 else (lambda i: (0, 0, 0))
    row = lambda i: (i, 0)
    const = lambda i: (0, 0)
    mod = pl.BlockSpec((None, 1, d), bidx)
    return pl.pallas_call(
        _outproj_kernel,
        grid=(t_tot // tm,),
        in_specs=[pl.BlockSpec((tm, 512), row)] * 3 + [
            pl.BlockSpec((tm, d), row), pl.BlockSpec((1536, d), const), mod,
            pl.BlockSpec((1, d), const), mod, mod, pl.BlockSpec((d, LANE), const)],
        out_specs=[pl.BlockSpec((tm, d), row), pl.BlockSpec((tm, d), row), pl.BlockSpec((tm, LANE), row)],
        out_shape=[jax.ShapeDtypeStruct((t_tot, d), F32), jax.ShapeDtypeStruct((t_tot, d), BF16),
                   jax.ShapeDtypeStruct((t_tot, LANE), F32)],
        compiler_params=_cparams(("arbitrary",)),
        name="outproj",
    )(ys, om, od, x2d, lw["w_out"], g1, lw["norm2_g"], sc2, sh2, lw["w_router"])


def _route_kernel(sco_ref, bias_ref, idx_ref, w_ref, rank_ref, cnt_ref, run_s):
    i = pl.program_id(0)
    tm = sco_ref.shape[0]
    gsz = N_EXPERTS // N_EXPERT_GROUPS

    @pl.when(i == 0)
    def _():
        run_s[...] = jnp.zeros(run_s.shape, F32)

    shape3 = (N_EXPERT_GROUPS, gsz, tm)
    sco = sco_ref[...]
    scores3 = sco.T[0:N_EXPERTS, :].reshape(shape3)
    sel3 = (sco + bias_ref[...]).T[0:N_EXPERTS, :].reshape(shape3)
    sub = lax.broadcasted_iota(jnp.int32, shape3, 1)
    gid = lax.broadcasted_iota(jnp.int32, shape3, 0)
    eid = gid * gsz + sub

    def max_all(a):
        return jnp.max(jnp.max(a, axis=0, keepdims=True), axis=1, keepdims=True)

    def min_all(a):
        return jnp.min(jnp.min(a, axis=0, keepdims=True), axis=1, keepdims=True)

    def sum_all(a):
        return jnp.sum(jnp.sum(a, axis=0, keepdims=True), axis=1, keepdims=True)

    m1 = jnp.max(sel3, axis=1, keepdims=True)
    first = jnp.min(jnp.where(sel3 == m1, sub, gsz), axis=1, keepdims=True)
    m2 = jnp.max(jnp.where(sub == first, NEG_INF, sel3), axis=1, keepdims=True)
    gscore = m1 + m2
    gid1 = lax.broadcasted_iota(jnp.int32, gscore.shape, 0)
    gsel = jnp.zeros(gscore.shape, F32)
    for _ in range(TOPK_GROUPS):
        gm = jnp.max(gscore, axis=0, keepdims=True)
        gfirst = jnp.min(jnp.where(gscore == gm, gid1, N_EXPERT_GROUPS), axis=0, keepdims=True)
        pick = gid1 == gfirst
        gsel = jnp.where(pick, 1.0, gsel)
        gscore = jnp.where(pick, NEG_INF, gscore)
    selm = jnp.where(gsel > 0.5, sel3, NEG_INF)
    chosen = jnp.zeros(shape3, F32)
    idxs, ws = [], []
    for _ in range(TOP_K):
        mx = max_all(selm)
        efirst = min_all(jnp.where(selm == mx, eid, N_EXPERTS))
        pick = eid == efirst
        chosen = jnp.where(pick, 1.0, chosen)
        idxs.append(efirst)
        ws.append(sum_all(jnp.where(pick, scores3, 0.0)))
        selm = jnp.where(pick, NEG_INF, selm)
    wsum = ws[0]
    for k in range(1, TOP_K):
        wsum = wsum + ws[k]
    r = lax.broadcasted_iota(jnp.int32, (tm, tm), 0)
    c = lax.broadcasted_iota(jnp.int32, (tm, tm), 1)
    before = jnp.where(r < c, 1.0, 0.0).astype(BF16)
    cm = chosen.reshape(N_EXPERTS, tm)
    run = run_s[...]
    excl3 = (_dot(cm.astype(BF16), before) + run[:, 0:1]).reshape(shape3)
    run_s[...] = run + jnp.sum(cm, axis=1, keepdims=True)
    for k in range(TOP_K):
        idx_ref[k:k + 1, :] = idxs[k].reshape(1, tm)
        w_ref[k:k + 1, :] = (ws[k] / wsum * ROUTED_SCALE).reshape(1, tm)
        rk = sum_all(jnp.where(eid == idxs[k], excl3, 0.0))
        rank_ref[k:k + 1, :] = rk.reshape(1, tm).astype(jnp.int32)
    cnt_ref[...] = run_s[...]


def _route(scores, bias_row):
    t_tot = scores.shape[0]
    tm = 512
    col = lambda i: (0, i)
    return pl.pallas_call(
        _route_kernel,
        grid=(t_tot // tm,),
        in_specs=[pl.BlockSpec((tm, LANE), lambda i: (i, 0)), pl.BlockSpec((1, LANE), lambda i: (0, 0))],
        out_specs=[pl.BlockSpec((TOP_K, tm), col), pl.BlockSpec((TOP_K, tm), col),
                   pl.BlockSpec((TOP_K, tm), col), pl.BlockSpec((N_EXPERTS, LANE), lambda i: (0, 0))],
        out_shape=[jax.ShapeDtypeStruct((TOP_K, t_tot), jnp.int32), jax.ShapeDtypeStruct((TOP_K, t_tot), F32),
                   jax.ShapeDtypeStruct((TOP_K, t_tot), jnp.int32),
                   jax.ShapeDtypeStruct((N_EXPERTS, LANE), F32)],
        scratch_shapes=[pltpu.VMEM((N_EXPERTS, LANE), F32)],
        compiler_params=_cparams(("arbitrary",)),
        name="route",
    )(scores, bias_row)


def _experts_kernel(bexp_ref, nused_ref, x_ref, wg_ref, wu_ref, wd_ref, y_ref):
    i = pl.program_id(0)

    @pl.when(i < nused_ref[0])
    def _():
        x = x_ref[...]
        hmid = _silu(_dot(x, wg_ref[...].astype(BF16))) * _dot(x, wu_ref[...].astype(BF16))
        y_ref[...] = _dot(hmid.astype(BF16), wd_ref[...].astype(BF16)).astype(BF16)

    @pl.when(i >= nused_ref[0])
    def _():
        y_ref[...] = jnp.zeros(y_ref.shape, BF16)


def _experts(x_sorted, block_exp, n_used, wg, wu, wd):
    slots, d = x_sorted.shape
    n_blocks = slots // EXPERT_ROWS
    hid = wg.shape[-1]
    return pl.pallas_call(
        _experts_kernel,
        grid_spec=pltpu.PrefetchScalarGridSpec(
            num_scalar_prefetch=2,
            grid=(n_blocks,),
            in_specs=[pl.BlockSpec((EXPERT_ROWS, d), lambda i, be, nu: (i, 0)),
                      pl.BlockSpec((None, d, hid), lambda i, be, nu: (be[i], 0, 0)),
                      pl.BlockSpec((None, d, hid), lambda i, be, nu: (be[i], 0, 0)),
                      pl.BlockSpec((None, hid, d), lambda i, be, nu: (be[i], 0, 0))],
            out_specs=pl.BlockSpec((EXPERT_ROWS, d), lambda i, be, nu: (i, 0))),
        out_shape=jax.ShapeDtypeStruct((slots, d), BF16),
        compiler_params=_cparams(("arbitrary",)),
        name="experts",
    )(block_exp, n_used, x_sorted, wg, wu, wd)


def _combine_kernel(final, h2_ref, yg_ref, w_ref, x1_ref, g2_ref, wsg_ref, wsu_ref, wsd_ref, fg_ref, *outs):
    h2 = h2_ref[...]
    d = h2.shape[1]
    shared = _dot((_silu(_dot(h2, wsg_ref[...])) * _dot(h2, wsu_ref[...])).astype(BF16), wsd_ref[...])
    w = w_ref[...]
    routed = yg_ref[:, 0:d].astype(F32) * w[:, 0:1]
    for k in range(1, TOP_K):
        routed = routed + yg_ref[:, k * d:(k + 1) * d].astype(F32) * w[:, k:k + 1]
    x2 = x1_ref[...] + g2_ref[...] * (routed + shared)
    if final:
        outs[0][...] = _rms(x2, fg_ref[...])
    else:
        outs[0][...] = x2


def _combine(h2, yg, w_tok, x1, g2, lw, final_g, seq_len, final):
    t_tot, d = x1.shape
    tm = min(256, seq_len)
    tpb = seq_len // tm
    nb = g2.shape[0]
    bidx = (lambda i: (i // tpb, 0, 0)) if nb > 1 else (lambda i: (0, 0, 0))
    row = lambda i: (i, 0)
    const = lambda i: (0, 0)
    hid = lw["w_s_gate"].shape[1]
    return pl.pallas_call(
        functools.partial(_combine_kernel, final),
        grid=(t_tot // tm,),
        in_specs=[pl.BlockSpec((tm, d), row), pl.BlockSpec((tm, TOP_K * d), row), pl.BlockSpec((tm, TOP_K), row),
                  pl.BlockSpec((tm, d), row), pl.BlockSpec((None, 1, d), bidx),
                  pl.BlockSpec((d, hid), const), pl.BlockSpec((d, hid), const), pl.BlockSpec((hid, d), const),
                  pl.BlockSpec((1, d), const)],
        out_specs=pl.BlockSpec((tm, d), row),
        out_shape=jax.ShapeDtypeStruct((t_tot, d), F32),
        compiler_params=_cparams(("arbitrary",)),
        name="combine_final" if final else "combine",
    )(h2, yg, w_tok, x1, g2, lw["w_s_gate"], lw["w_s_up"], lw["w_s_down"], final_g)


def _rope_tables(seq_len):
    t = jnp.arange(seq_len, dtype=jnp.int32)
    rows = (t // GRID_W).astype(F32)[:, None]
    cols = (t % GRID_W).astype(F32)[:, None]
    lane = jnp.arange(LANE)

    def build(rel, width, active):
        dr = width // 2
        half = dr // 2
        within = rel % dr
        freq = ROPE_BASE ** (-(within % half).astype(F32) / half)
        pos = jnp.where((rel < dr)[None, :], rows, cols)
        ang = pos * freq[None, :]
        sign = jnp.where(within < half, -1.0, 1.0)
        cos = jnp.where(active[None, :], jnp.cos(ang), 1.0)
        sin = jnp.where(active[None, :], jnp.sin(ang) * sign[None, :], 0.0)
        return cos.astype(F32), sin.astype(F32)

    act_q = (lane >= _KROPE_LANE) & (lane < _KROPE_LANE + MLA_ROPE_DIM)
    cq, sq = build(jnp.clip(lane - _KROPE_LANE, 0, MLA_ROPE_DIM - 1), MLA_ROPE_DIM, act_q)
    cd, sd = build(lane % DIFF_HEAD_DIM, DIFF_HEAD_DIM, jnp.ones((LANE,), bool))
    return cq, sq, cd, sd


def _prep_layer(p, l):
    w_in = p["w_in"][l]
    d = w_in.shape[0]
    z0, x0, t0, q0, kv0, dq0, dk0, dv0 = 0, 512, 1536, 1552, 1808, 1968, 2480, 2992
    misc = jnp.zeros((d, LANE), F32)
    misc = misc.at[:, 0:2 * SSD_HEADS].set(w_in[:, t0:q0])
    misc = misc.at[:, _KROPE_LANE:_KROPE_LANE + MLA_ROPE_DIM].set(w_in[:, kv0 + MLA_KV_RANK:dq0])
    w_cat = jnp.concatenate([w_in[:, z0:x0], w_in[:, x0:t0], misc, w_in[:, q0:kv0],
                             w_in[:, kv0:kv0 + MLA_KV_RANK], w_in[:, dq0:dk0], w_in[:, dk0:dv0],
                             w_in[:, dv0:]], axis=1).astype(BF16)
    qd = MLA_NOPE + MLA_ROPE_DIM
    w_qb = jnp.pad(p["w_q_b"][l].reshape(MLA_Q_RANK, MLA_HEADS, qd), ((0, 0), (0, 0), (0, LANE - qd)))
    w_kvb = p["w_kv_b"][l].reshape(MLA_KV_RANK, MLA_HEADS, MLA_NOPE + MLA_V)
    w_k = jnp.pad(w_kvb[:, :, :MLA_NOPE], ((0, 0), (0, 0), (0, LANE - MLA_NOPE)))
    vv = w_kvb[:, :, MLA_NOPE:]
    zero = jnp.zeros_like(vv)
    even = (jnp.arange(MLA_HEADS) % 2 == 0)[None, :, None]
    w_v = jnp.concatenate([jnp.where(even, vv, zero), jnp.where(even, zero, vv)], axis=2)
    pad_lane = lambda a: jnp.pad(a, ((0, 0), (0, LANE - a.shape[1])))
    lamv = jnp.stack([p["lambda_q1"][l], p["lambda_k1"][l], p["lambda_q2"][l], p["lambda_k2"][l]])
    return {
        "w_in": w_cat,
        "w_qb": w_qb.reshape(MLA_Q_RANK, MLA_HEADS * LANE).astype(BF16),
        "w_k": w_k.reshape(MLA_KV_RANK, MLA_HEADS * LANE).astype(BF16),
        "w_v": w_v.reshape(MLA_KV_RANK, MLA_HEADS * LANE).astype(BF16),
        "gqa": p["q_a_norm_g"][l][None, :], "gkv": p["kv_a_norm_g"][l][None, :],
        "norm1_g": p["norm1_g"][l][None, :], "norm2_g": p["norm2_g"][l][None, :],
        "conv_w": p["conv_w"][l], "conv_b": p["conv_b"][l][None, :],
        "dt_bias": pad_lane(p["dt_bias"][l].reshape(1, -1)), "a_log": pad_lane(p["a_log"][l].reshape(1, -1)),
        "d_skip": jnp.repeat(p["d_skip"][l], SSD_HEAD_DIM)[None, :], "ssd_norm_g": p["ssd_norm_g"][l][None, :],
        "lamv": pad_lane(lamv), "subln_g": p["diff_subln_g"][l][None, :],
        "w_out": p["w_out"][l].astype(BF16),
        "w_router": pad_lane(p["w_router"][l]).astype(BF16),
        "router_bias": pad_lane(p["router_bias"][l][None, :]),
        "w_e_gate": p["w_e_gate"][l], "w_e_up": p["w_e_up"][l], "w_e_down": p["w_e_down"][l],
        "w_s_gate": p["w_s_gate"][l].astype(BF16), "w_s_up": p["w_s_up"][l].astype(BF16),
        "w_s_down": p["w_s_down"][l].astype(BF16),
    }


def _lambda_init(layer_idx):
    return 0.8 - 0.6 * math.exp(-0.3 * layer_idx)


def _moe(h2, scores, x1, g2, lw, final_g, seq_len, final):
    t_tot, d = h2.shape
    idx_t, w_t, rank_t, cnt = _route(scores, lw["router_bias"])
    counts = cnt[:, 0].astype(jnp.int32)
    padded = (counts + EXPERT_ROWS - 1) // EXPERT_ROWS * EXPERT_ROWS
    pad_end = jnp.cumsum(padded)
    pad_start = pad_end - padded
    dest = pad_start[idx_t] + rank_t
    n_blocks = t_tot * TOP_K // EXPERT_ROWS + N_EXPERTS
    slots = n_blocks * EXPERT_ROWS
    tok = jnp.broadcast_to(jnp.arange(t_tot, dtype=jnp.int32)[None, :], dest.shape)
    slot_tok = jnp.zeros((slots,), jnp.int32).at[dest.reshape(-1)].set(tok.reshape(-1))
    block_start = jnp.arange(n_blocks, dtype=jnp.int32) * EXPERT_ROWS
    block_exp = jnp.minimum(jnp.searchsorted(pad_end, block_start, side="right"), N_EXPERTS - 1).astype(jnp.int32)
    n_used = (pad_end[-1:] // EXPERT_ROWS).astype(jnp.int32)
    x_sorted = jnp.take(h2, slot_tok, axis=0)
    y_sorted = _experts(x_sorted, block_exp, n_used, lw["w_e_gate"], lw["w_e_up"], lw["w_e_down"])
    yg = jnp.take(y_sorted, dest.T.reshape(-1), axis=0).reshape(t_tot, TOP_K * d)
    return _combine(h2, yg, w_t.T, x1, g2, lw, final_g, seq_len, final)


def _block(x2d, mod, lw, layer_idx, n, seq_len, tables, cached, final_g, final):
    d = x2d.shape[1]
    sh1, sc1, g1, sh2, sc2, g2 = [mod[:, i:i + 1, :] for i in range(6)]
    latent = cached is not None
    res = _inproj(x2d, sc1, sh1, lw["norm1_g"], lw, seq_len, tables)
    z, xbc, misc, q, k, v, dq, dk, dv = res[:9]
    lam_init = _lambda_init(layer_idx)
    if latent:
        past = cached["ckv"].shape[1]
        y_ssd, _ = _ssd(xbc, misc, z, lw, n, seq_len, cached["ssd"], False)
        kr_pad = jnp.pad(cached["krope"].reshape(n * past, MLA_ROPE_DIM),
                         ((0, 0), (_KROPE_LANE, LANE - _KROPE_LANE - MLA_ROPE_DIM)))
        kc, vc = _ctxkv(cached["ckv"].reshape(n * past, MLA_KV_RANK), kr_pad, lw)
        o_mla = _attention("mla", q, k, v, kc, vc, n, seq_len, past, None)
        dkc = cached["dk"].reshape(n * past, DIFF_WIDTH).astype(BF16)
        dvc = cached["dv"].reshape(n * past, DIFF_WIDTH).astype(BF16)
        o_diff = _attention("diff", dq, dk, dv, dkc, dvc, n, seq_len, past, (lw["lamv"], lw["subln_g"]), lam_init)
        ctx_new = None
    else:
        y_ssd, st = _ssd(xbc, misc, z, lw, n, seq_len, None, True)
        o_mla = _attention("mla", q, k, v, None, None, n, seq_len, 0, None)
        o_diff = _attention("diff", dq, dk, dv, None, None, n, seq_len, 0, (lw["lamv"], lw["subln_g"]), lam_init)
        ckv_f, dk_f, dv_f = res[9:]
        ctx_new = (st.reshape(n, 2, SSD_HEADS, SSD_HEAD_DIM, SSD_STATE),
                   ckv_f.reshape(n, seq_len, MLA_KV_RANK),
                   misc[:, _KROPE_LANE:_KROPE_LANE + MLA_ROPE_DIM].reshape(n, seq_len, MLA_ROPE_DIM),
                   dk_f.reshape(n, seq_len, DIFF_HEADS, 2 * DIFF_HEAD_DIM),
                   dv_f.reshape(n, seq_len, DIFF_HEADS, 2 * DIFF_HEAD_DIM))
    x1, h2, scores = _outproj(y_ssd, o_mla, o_diff, x2d, g1, sc2, sh2, lw, seq_len)
    x2 = _moe(h2, scores, x1, g2, lw, final_g, seq_len, final)
    return x2, ctx_new


def kernel(x_prompt, x_sample, state_ssd, cache_mla_ckv, cache_mla_krope, cache_diff_k, cache_diff_v, c, c_ctx, w_ada, b_ada, norm1_g, norm2_g, w_in, conv_w, conv_b, dt_bias, a_log, d_skip, ssd_norm_g, q_a_norm_g, w_q_b, kv_a_norm_g, w_kv_b, lambda_q1, lambda_k1, lambda_q2, lambda_k2, diff_subln_g, w_out, w_router, router_bias, w_e_gate, w_e_up, w_e_down, w_s_gate, w_s_up, w_s_down, final_norm_g):
    p = dict(norm1_g=norm1_g, norm2_g=norm2_g, w_in=w_in, conv_w=conv_w, conv_b=conv_b, dt_bias=dt_bias,
             a_log=a_log, d_skip=d_skip, ssd_norm_g=ssd_norm_g, q_a_norm_g=q_a_norm_g, w_q_b=w_q_b,
             kv_a_norm_g=kv_a_norm_g, w_kv_b=w_kv_b, lambda_q1=lambda_q1, lambda_k1=lambda_k1,
             lambda_q2=lambda_q2, lambda_k2=lambda_k2, diff_subln_g=diff_subln_g, w_out=w_out,
             w_router=w_router, router_bias=router_bias, w_e_gate=w_e_gate, w_e_up=w_e_up, w_e_down=w_e_down,
             w_s_gate=w_s_gate, w_s_up=w_s_up, w_s_down=w_s_down)
    depth = w_in.shape[0]
    nc, lc, d = x_prompt.shape
    ns, ls, _ = x_sample.shape
    cond_rows = 16
    cond = jnp.zeros((cond_rows, d), F32).at[0:ns].set(c).at[ns].set(c_ctx)
    mod = _ada(cond, w_ada, b_ada).reshape(depth, cond_rows, 6, d)
    tables = _rope_tables(ls)
    final_g = final_norm_g[None, :]
    xp = x_prompt.reshape(nc * lc, d)
    xs = x_sample.reshape(ns * ls, d)
    news = []
    for l in range(depth):
        lw = _prep_layer(p, l)
        final = l == depth - 1
        xp, ctx_new = _block(xp, mod[l, ns:ns + 1], lw, l, nc, lc, None, None, final_g, final)
        news.append(ctx_new)
        cached = {"ssd": state_ssd[:, l].reshape(ns, 2, SSD_GROUPS, 4 * SSD_HEAD_DIM, SSD_STATE),
                  "ckv": cache_mla_ckv[:, l], "krope": cache_mla_krope[:, l],
                  "dk": cache_diff_k[:, l], "dv": cache_diff_v[:, l]}
        xs, _ = _block(xs, mod[l, 0:ns], lw, l, ns, ls, tables, cached, final_g, final)
    stack = lambda i: jnp.stack([nw[i] for nw in news], axis=1)
    return (xp.reshape(nc, lc, d), xs.reshape(ns, ls, d), stack(0), stack(1), stack(2), stack(3), stack(4))
```

```python
import functools
import math

import jax
import jax.numpy as jnp
from jax import lax
from jax.experimental import pallas as pl
from jax.experimental.pallas import tpu as pltpu

F32 = jnp.float32
BF16 = jnp.bfloat16

GRID_W = 64
ROPE_BASE = 10000.0
EPS = 1e-6
SSD_HEADS = 8
SSD_HEAD_DIM = 64
SSD_WIDTH = 512
SSD_GROUPS = 2
SSD_STATE = 128
SSD_CHUNK = 128
SSD_CONV_CH = 1024
MLA_HEADS = 8
MLA_Q_RANK = 256
MLA_KV_RANK = 128
MLA_NOPE = 64
MLA_ROPE_DIM = 32
MLA_V = 64
MLA_SCALE = (MLA_NOPE + MLA_ROPE_DIM) ** -0.5
DIFF_HEADS = 4
DIFF_HEAD_DIM = 64
DIFF_WIDTH = 512
DIFF_SCALE = DIFF_HEAD_DIM ** -0.5
N_EXPERTS = 64
N_EXPERT_GROUPS = 8
TOPK_GROUPS = 4
TOP_K = 8
EXPERT_HIDDEN = 256
ROUTED_SCALE = 2.5

LANE = 128
VMEM_LIMIT = 56 * 1024 * 1024
EXPERT_ROWS = 512
ATTN_HEADS = 8
ATTN_STRIP = 16
NEG_INF = float("-inf")
LOG2E = math.log2(math.e)


def _cparams(sem, **kw):
    return pltpu.CompilerParams(dimension_semantics=sem, vmem_limit_bytes=VMEM_LIMIT, **kw)


def _silu(x):
    return x * (1.0 / (1.0 + jnp.exp(-x)))


def _softplus(x):
    return jnp.maximum(x, 0.0) + jnp.log1p(jnp.exp(-jnp.abs(x)))


def _rms(x, g):
    return x * lax.rsqrt(jnp.mean(x * x, axis=-1, keepdims=True) + EPS) * g


def _dot(a, b):
    return jnp.dot(a, b, preferred_element_type=F32)


def _dot_nt(a, b):
    return lax.dot_general(a, b, (((1,), (1,)), ((), ())), preferred_element_type=F32)


def _lane_iota(shape):
    return lax.broadcasted_iota(jnp.int32, shape, len(shape) - 1)


def _with_ones_lane(v):
    return jnp.where((_lane_iota(v.shape) & (LANE - 1)) == MLA_V, 1.0, v)


def _rot_half(x, half):
    lane = _lane_iota(x.shape)
    up = pltpu.roll(x, LANE - half, axis=1)
    dn = pltpu.roll(x, half, axis=1)
    return jnp.where((lane & (2 * half - 1)) < half, up, dn)


def _ada_kernel(c_ref, w_ref, b_ref, o_ref):
    s = _silu(c_ref[...])
    o_ref[...] = _dot(s.astype(BF16), w_ref[...].astype(BF16)) + b_ref[...]


def _ada(cond, w_ada, b_ada):
    depth, d, n6 = w_ada.shape
    rows = cond.shape[0]
    tn = 1536
    return pl.pallas_call(
        _ada_kernel,
        grid=(depth, n6 // tn),
        in_specs=[pl.BlockSpec((rows, d), lambda l, j: (0, 0)),
                  pl.BlockSpec((None, d, tn), lambda l, j: (l, 0, j)),
                  pl.BlockSpec((None, 1, tn), lambda l, j: (l, 0, j))],
        out_specs=pl.BlockSpec((None, rows, tn), lambda l, j: (l, 0, j)),
        out_shape=jax.ShapeDtypeStruct((depth, rows, n6), F32),
        compiler_params=_cparams(("arbitrary", "arbitrary")),
        name="ada",
    )(cond, w_ada, b_ada.reshape(depth, 1, n6))


_C_Z, _C_XBC, _C_MISC, _C_QA, _C_CKV, _C_DQ, _C_DK, _C_DV, _C_END = (
    0, 512, 1536, 1664, 1920, 2048, 2560, 3072, 3584)
_KROPE_LANE = 64


def _inproj_kernel(latent, x_ref, sc_ref, sh_ref, g1_ref, w_ref, gqa_ref, gkv_ref, wqb_ref, wk_ref, wv_ref,
                   *rest):
    if latent:
        (cq_ref, sq_ref, cd_ref, sd_ref, _, _, _, _,
         z_ref, xbc_ref, misc_ref, q_ref, dq_ref, k_ref, v_ref, dk_ref, dv_ref) = rest
    else:
        (z_ref, xbc_ref, misc_ref, q_ref, dq_ref, k_ref, v_ref, dk_ref, dv_ref,
         ckv_ref, dkf_ref, dvf_ref) = rest
    x = x_ref[...]
    h = (_rms(x, g1_ref[...]) * (1.0 + sc_ref[...]) + sh_ref[...]).astype(BF16)

    def seg(a, b):
        return _dot(h, w_ref[:, a:b])

    z_ref[...] = seg(_C_Z, _C_XBC).astype(BF16)
    xbc_ref[...] = seg(_C_XBC, _C_MISC).astype(BF16)
    misc = seg(_C_MISC, _C_QA)
    misc_ref[...] = misc

    qn = _rms(seg(_C_QA, _C_CKV), gqa_ref[...]).astype(BF16)
    q = _dot(qn, wqb_ref[...]) * (MLA_SCALE * LOG2E)
    ckv = _rms(seg(_C_CKV, _C_DQ), gkv_ref[...])
    ckv_b = ckv.astype(BF16)
    kn = _dot(ckv_b, wk_ref[...])
    v_ref[...] = _with_ones_lane(_dot(ckv_b, wv_ref[...])).astype(BF16)
    lane = _lane_iota(misc.shape)
    in_rope = (lane >= _KROPE_LANE) & (lane < _KROPE_LANE + MLA_ROPE_DIM)
    if latent:
        cq, sq = cq_ref[...], sq_ref[...]
        kr = jnp.where(in_rope, misc * cq + _rot_half(misc, MLA_ROPE_DIM // 4) * sq, 0.0)
    else:
        kr = jnp.where(in_rope, misc, 0.0)
    for hd in range(MLA_HEADS):
        sl = slice(hd * LANE, (hd + 1) * LANE)
        qh = q[:, sl]
        if latent:
            qh = qh * cq + _rot_half(qh, MLA_ROPE_DIM // 4) * sq
        q_ref[:, sl] = qh.astype(BF16)
        k_ref[:, sl] = (kn[:, sl] + kr).astype(BF16)

    dq = seg(_C_DQ, _C_DK) * (DIFF_SCALE * LOG2E)
    dk = seg(_C_DK, _C_DV)
    dv = seg(_C_DV, _C_END)
    dv_ref[...] = dv.astype(BF16)
    lo = lane < DIFF_HEAD_DIM
    if latent:
        cd, sd = cd_ref[...], sd_ref[...]
    for hd in range(DIFF_HEADS):
        sl = slice(hd * LANE, (hd + 1) * LANE)
        a = dq[:, sl]
        b = dk[:, sl]
        if latent:
            a = a * cd + _rot_half(a, DIFF_HEAD_DIM // 4) * sd
            b = b * cd + _rot_half(b, DIFF_HEAD_DIM // 4) * sd
        dq_ref[:, 2 * hd * LANE:(2 * hd + 1) * LANE] = jnp.where(lo, a, 0.0).astype(BF16)
        dq_ref[:, (2 * hd + 1) * LANE:(2 * hd + 2) * LANE] = jnp.where(lo, 0.0, a).astype(BF16)
        dk_ref[:, sl] = b.astype(BF16)
    if not latent:
        ckv_ref[...] = ckv
        dkf_ref[...] = dk
        dvf_ref[...] = dv


def _inproj(x2d, sc, sh, g1, lw, seq_len, tables, kv_bufs, past_len):
    t_tot, d = x2d.shape
    latent = tables is not None
    tm = min(512, seq_len)
    tpb = seq_len // tm
    ppb = past_len // tm if latent else 0
    nb = sc.shape[0]
    bidx = (lambda i: (i // tpb, 0, 0)) if nb > 1 else (lambda i: (0, 0, 0))
    row = lambda i: (i, 0)
    kvrow = lambda i: ((i // tpb) * (tpb + ppb) + i % tpb, 0)
    const = lambda i: (0, 0)
    in_specs = [pl.BlockSpec((tm, d), row),
                pl.BlockSpec((None, 1, d), bidx), pl.BlockSpec((None, 1, d), bidx),
                pl.BlockSpec((1, d), const),
                pl.BlockSpec((d, _C_END), const),
                pl.BlockSpec((1, MLA_Q_RANK), const), pl.BlockSpec((1, MLA_KV_RANK), const),
                pl.BlockSpec((MLA_Q_RANK, MLA_HEADS * LANE), const),
                pl.BlockSpec((MLA_KV_RANK, MLA_HEADS * LANE), const),
                pl.BlockSpec((MLA_KV_RANK, MLA_HEADS * LANE), const)]
    args = [x2d, sc, sh, g1, lw["w_in"], lw["gqa"], lw["gkv"], lw["w_qb"], lw["w_k"], lw["w_v"]]
    aliases = {}
    if latent:
        tab = lambda i: (i % tpb, 0)
        in_specs += [pl.BlockSpec((tm, LANE), tab)] * 4
        args += list(tables)
        for j, buf in enumerate(kv_bufs):
            aliases[len(args)] = 5 + j
            in_specs.append(pl.BlockSpec(memory_space=pl.ANY))
            args.append(buf)
    kv_rows = (t_tot // seq_len) * (seq_len + past_len) if latent else t_tot

    def o(width, dtype, kv=False):
        return (pl.BlockSpec((tm, width), kvrow if kv else row),
                jax.ShapeDtypeStruct((kv_rows if kv else t_tot, width), dtype))

    outs = [o(512, BF16), o(1024, BF16), o(LANE, F32), o(1024, BF16), o(1024, BF16),
            o(1024, BF16, True), o(1024, BF16, True), o(512, BF16, True), o(512, BF16, True)]
    if not latent:
        outs += [o(MLA_KV_RANK, F32), o(512, F32), o(512, F32)]
    return pl.pallas_call(
        functools.partial(_inproj_kernel, latent),
        grid=(t_tot // tm,),
        in_specs=in_specs,
        out_specs=[s for s, _ in outs],
        out_shape=[s for _, s in outs],
        input_output_aliases=aliases,
        compiler_params=_cparams(("arbitrary",)),
        name="inproj_lat" if latent else "inproj_ctx",
    )(*args)


def _ctxkv_kernel(ckv_ref, kr_ref, dkc_ref, dvc_ref, wk_ref, wv_ref, _k0, _v0, _dk0, _dv0,
                  k_ref, v_ref, dk_ref, dv_ref):
    c = ckv_ref[...].astype(BF16)
    kn = _dot(c, wk_ref[...])
    kr = kr_ref[...]
    for hd in range(MLA_HEADS):
        sl = slice(hd * LANE, (hd + 1) * LANE)
        k_ref[:, sl] = (kn[:, sl] + kr).astype(BF16)
    v_ref[...] = _with_ones_lane(_dot(c, wv_ref[...])).astype(BF16)
    dk_ref[...] = dkc_ref[...].astype(BF16)
    dv_ref[...] = dvc_ref[...].astype(BF16)


def _ctxkv(ckv2d, kr_pad, dkc, dvc, lw, n, seq_len, past_len):
    tm = min(512, seq_len)
    tpb, ppb = seq_len // tm, past_len // tm
    row = lambda i: (i, 0)
    orow = lambda i: ((i // ppb) * (tpb + ppb) + tpb + i % ppb, 0)
    const = lambda i: (0, 0)
    w = MLA_HEADS * LANE
    rows = n * (seq_len + past_len)
    shapes = [jax.ShapeDtypeStruct((rows, w), BF16)] * 2 + [jax.ShapeDtypeStruct((rows, DIFF_WIDTH), BF16)] * 2
    blanks = [jnp.zeros(s.shape, s.dtype) for s in shapes]
    return pl.pallas_call(
        _ctxkv_kernel,
        grid=(n * ppb,),
        in_specs=[pl.BlockSpec((tm, MLA_KV_RANK), row), pl.BlockSpec((tm, LANE), row),
                  pl.BlockSpec((tm, DIFF_WIDTH), row), pl.BlockSpec((tm, DIFF_WIDTH), row),
                  pl.BlockSpec((MLA_KV_RANK, w), const), pl.BlockSpec((MLA_KV_RANK, w), const)]
        + [pl.BlockSpec(memory_space=pl.ANY)] * 4,
        out_specs=[pl.BlockSpec((tm, w), orow), pl.BlockSpec((tm, w), orow),
                   pl.BlockSpec((tm, DIFF_WIDTH), orow), pl.BlockSpec((tm, DIFF_WIDTH), orow)],
        out_shape=shapes,
        input_output_aliases={6: 0, 7: 1, 8: 2, 9: 3},
        compiler_params=_cparams(("arbitrary",)),
        name="ctxkv",
    )(ckv2d, kr_pad, dkc, dvc, lw["w_k"], lw["w_v"], *blanks)


_HALO = 16


def _ssd_kernel(nblk, tb, has_init, want_state, *refs):
    it = iter(refs)
    xbc_ref, prev_ref, next_ref, misc_ref, z_ref = [next(it) for _ in range(5)]
    cw_ref, cb_ref, dtb_ref, alog_ref, dsk_ref, gn_ref = [next(it) for _ in range(6)]
    init_ref = next(it) if has_init else None
    y_ref = next(it)
    st_ref = next(it) if want_state else None
    yf_s, s_s, xc_s, dt_s, da_s = [next(it) for _ in range(5)]

    j = pl.program_id(1)
    nch = tb // SSD_CHUNK
    blk = jnp.where(j < nblk, j, 2 * nblk - 1 - j)

    xb = xbc_ref[...].astype(F32)
    rowid = lax.broadcasted_iota(jnp.int32, xb.shape, 0)
    prev_row = prev_ref[_HALO - 1:_HALO, :].astype(F32) * (blk > 0).astype(F32)
    next_row = next_ref[0:1, :].astype(F32) * (blk < nblk - 1).astype(F32)
    xm1 = jnp.where(rowid == 0, prev_row, pltpu.roll(xb, 1, axis=0))
    xp1 = jnp.where(rowid == tb - 1, next_row, pltpu.roll(xb, tb - 1, axis=0))
    cw = cw_ref[...]
    conv = xm1 * cw[0:1, :] + xb * cw[1:2, :] + xp1 * cw[2:3, :] + cb_ref[...]
    xc_s[...] = _silu(conv)
    dt = _softplus(misc_ref[...] + dtb_ref[...])
    dt_s[...] = dt
    da_s[...] = dt * (-jnp.exp(alog_ref[...]))

    def load_init(d):
        if has_init:
            s_s[...] = init_ref[d]
        else:
            s_s[...] = jnp.zeros(s_s.shape, F32)

    @pl.when(j == 0)
    def _():
        load_init(0)

    @pl.when(j == nblk)
    def _():
        if want_state:
            st_ref[0] = s_s[...]
        load_init(1)

    ri = lax.broadcasted_iota(jnp.int32, (SSD_CHUNK, SSD_CHUNK), 0)
    ci = lax.broadcasted_iota(jnp.int32, (SSD_CHUNK, SSD_CHUNK), 1)
    lane = ci
    lo = lane < SSD_HEAD_DIM

    def bcol(m, li):
        return jnp.broadcast_to(m[:, li:li + 1], (SSD_CHUNK, SSD_CHUNK))

    def pair_cols(m, li):
        return jnp.where(lo, bcol(m, li), bcol(m, li + 1))

    def chunk_step(c, d):
        r0 = pl.multiple_of(c * SSD_CHUNK, SSD_CHUNK)
        rows = pl.ds(r0, SSD_CHUNK)
        mask = (ri >= ci) if d == 0 else (ri <= ci)
        da_c = da_s[rows, :]
        acum = jnp.dot(mask.astype(F32), da_c, preferred_element_type=F32, precision=lax.Precision.HIGHEST)
        acum_t = acum.T
        tot = acum[SSD_CHUNK - 1:SSD_CHUNK, :] if d == 0 else acum[0:1, :]
        eac = jnp.exp(acum)
        wdec = jnp.exp(tot - acum)
        etot = jnp.exp(tot)
        dt_c = dt_s[rows, :]
        xs = xc_s[rows, 0:SSD_WIDTH]
        y_parts = []
        for g in range(SSD_GROUPS):
            b_g = xc_s[rows, SSD_WIDTH + g * SSD_STATE:SSD_WIDTH + (g + 1) * SSD_STATE].astype(BF16)
            c0 = SSD_WIDTH + SSD_GROUPS * SSD_STATE + g * SSD_STATE
            c_g = xc_s[rows, c0:c0 + SSD_STATE].astype(BF16)
            cb = _dot_nt(c_g, b_g)
            s_g = s_s[g]
            yoff = _dot_nt(c_g, s_g.astype(BF16))
            xw_parts = []
            for pp in range(2):
                h0 = 4 * g + 2 * pp
                li = SSD_HEADS * d + h0
                sc = []
                for e in range(2):
                    col = bcol(acum, li + e)
                    rowb = jnp.broadcast_to(acum_t[li + e:li + e + 1, :], (SSD_CHUNK, SSD_CHUNK))
                    lm = jnp.exp(jnp.where(mask, col - rowb, NEG_INF))
                    sc.append((cb * lm).astype(BF16))
                scp = jnp.concatenate(sc, axis=1)
                xdt = xs[:, LANE * (2 * g + pp):LANE * (2 * g + pp + 1)] * pair_cols(dt_c, li)
                bd = jnp.concatenate([jnp.where(lo, xdt, 0.0), jnp.where(lo, 0.0, xdt)], axis=0).astype(BF16)
                ydiag = _dot(scp, bd)
                y_parts.append(ydiag + yoff[:, LANE * pp:LANE * (pp + 1)] * pair_cols(eac, li))
                xw_parts.append(xdt * pair_cols(wdec, li))
            xw_t = jnp.concatenate(xw_parts, axis=1).T.astype(BF16)
            ds = _dot(xw_t, b_g)
            dec = jnp.concatenate(
                [jnp.broadcast_to(etot[:, SSD_HEADS * d + 4 * g + hh:SSD_HEADS * d + 4 * g + hh + 1],
                                  (SSD_HEAD_DIM, SSD_STATE)) for hh in range(4)], axis=0)
            s_s[g] = s_g * dec + ds
        y_c = jnp.concatenate(y_parts, axis=1)
        arow = pl.ds(pl.multiple_of(blk * tb + r0, SSD_CHUNK), SSD_CHUNK)
        if d == 0:
            yf_s[arow, :] = y_c + xs * dsk_ref[...]
        else:
            y = yf_s[arow, :] + y_c
            gated = y * _silu(z_ref[rows, :].astype(F32))
            y_ref[rows, :] = _rms(gated, gn_ref[...]).astype(BF16)

    @pl.when(j < nblk)
    def _():
        def body(c, carry):
            chunk_step(c, 0)
            return carry
        lax.fori_loop(0, nch, body, 0)

    @pl.when(j >= nblk)
    def _():
        def body(c, carry):
            chunk_step(nch - 1 - c, 1)
            return carry
        lax.fori_loop(0, nch, body, 0)

    if want_state:
        @pl.when(j == 2 * nblk - 1)
        def _():
            st_ref[1] = s_s[...]


def _ssd(xbc, misc, z, lw, n, seq_len, init, want_state):
    tb = min(512, seq_len)
    nblk = seq_len // tb
    hpb = tb // _HALO
    has_init = init is not None

    def blk_of(j):
        return jnp.where(j < nblk, j, 2 * nblk - 1 - j)

    main = lambda b, j: (b * nblk + blk_of(j), 0)
    prev = lambda b, j: (jnp.maximum((b * nblk + blk_of(j)) * hpb - 1, 0), 0)
    nxt = lambda b, j: (jnp.minimum((b * nblk + blk_of(j) + 1) * hpb, n * nblk * hpb - 1), 0)
    const = lambda b, j: (0, 0)
    in_specs = [pl.BlockSpec((tb, SSD_CONV_CH), main),
                pl.BlockSpec((_HALO, SSD_CONV_CH), prev),
                pl.BlockSpec((_HALO, SSD_CONV_CH), nxt),
                pl.BlockSpec((tb, LANE), main),
                pl.BlockSpec((tb, SSD_WIDTH), main),
                pl.BlockSpec((3, SSD_CONV_CH), const), pl.BlockSpec((1, SSD_CONV_CH), const),
                pl.BlockSpec((1, LANE), const), pl.BlockSpec((1, LANE), const),
                pl.BlockSpec((1, SSD_WIDTH), const), pl.BlockSpec((1, SSD_WIDTH), const)]
    args = [xbc, xbc, xbc, misc, z, lw["conv_w"], lw["conv_b"], lw["dt_bias"], lw["a_log"], lw["d_skip"],
            lw["ssd_norm_g"]]
    st_block = (None, 2, SSD_GROUPS, 4 * SSD_HEAD_DIM, SSD_STATE)
    if has_init:
        in_specs.append(pl.BlockSpec(st_block, lambda b, j: (b, 0, 0, 0, 0)))
        args.append(init)
    omap = lambda b, j: (b * nblk + jnp.where(j < nblk, nblk - 1, 2 * nblk - 1 - j), 0)
    out_specs = [pl.BlockSpec((tb, SSD_WIDTH), omap)]
    out_shape = [jax.ShapeDtypeStruct((n * seq_len, SSD_WIDTH), BF16)]
    if want_state:
        out_specs.append(pl.BlockSpec(st_block, lambda b, j: (b, 0, 0, 0, 0)))
        out_shape.append(jax.ShapeDtypeStruct((n,) + st_block[1:], F32))
    res = pl.pallas_call(
        functools.partial(_ssd_kernel, nblk, tb, has_init, want_state),
        grid=(n, 2 * nblk),
        in_specs=in_specs,
        out_specs=out_specs,
        out_shape=out_shape,
        scratch_shapes=[pltpu.VMEM((seq_len, SSD_WIDTH), F32),
                        pltpu.VMEM((SSD_GROUPS, 4 * SSD_HEAD_DIM, SSD_STATE), F32),
                        pltpu.VMEM((tb, SSD_CONV_CH), F32),
                        pltpu.VMEM((tb, LANE), F32),
                        pltpu.VMEM((tb, LANE), F32)],
        compiler_params=_cparams(("arbitrary", "arbitrary")),
        name="ssd_lat" if has_init else "ssd_ctx",
    )(*args)
    return res if want_state else (res[0], None)


def _attn_kernel(kind, lam_init, *refs):
    if kind == "mla":
        q_ref, k_ref, v_ref, o_ref, m_ref, acc_ref, s_scr, p_scr, a_scr = refs
    else:
        q_ref, k_ref, v_ref, lamv_ref, g_ref, o_ref, m_ref, acc_ref, s_scr, p_scr, a_scr, l_ref = refs
    ki = pl.program_id(2)
    nk = pl.num_programs(2)
    tq, tk = s_scr.shape[1], s_scr.shape[2]

    @pl.when(ki == 0)
    def _():
        m_ref[...] = jnp.full(m_ref.shape, NEG_INF, F32)
        acc_ref[...] = jnp.zeros(acc_ref.shape, F32)
        if kind != "mla":
            l_ref[...] = jnp.zeros(l_ref.shape, F32)

    def kv_lanes(vh):
        kv = vh if kind == "mla" else vh // 2
        return slice(kv * LANE, (kv + 1) * LANE)

    def scores(vh):
        s_scr[vh] = _dot_nt(q_ref[:, vh * LANE:(vh + 1) * LANE], k_ref[:, kv_lanes(vh)])

    scores(0)
    for vh in range(ATTN_HEADS):
        ksl = kv_lanes(vh)
        if vh + 1 < ATTN_HEADS:
            scores(vh + 1)
        for i in range(tq // ATTN_STRIP):
            r = slice(i * ATTN_STRIP, (i + 1) * ATTN_STRIP)
            s = s_scr[vh, r, :]
            m_prev = m_ref[vh, r, :]
            m_new = jnp.maximum(m_prev, jnp.max(s, axis=-1, keepdims=True))
            alpha = jnp.exp2(m_prev - m_new)
            p = jnp.exp2(s - jnp.concatenate([m_new] * (tk // LANE), axis=1))
            if kind != "mla":
                l_ref[vh, r, :] = alpha * l_ref[vh, r, :] + jnp.sum(p, axis=-1, keepdims=True)
            m_ref[vh, r, :] = m_new
            a_scr[vh, r, :] = alpha
            p_scr[vh, r, :] = p.astype(BF16)
        acc_ref[vh] = acc_ref[vh] * a_scr[vh] + _dot(p_scr[vh], v_ref[:, ksl])

    @pl.when(ki == nk - 1)
    def _():
        if kind == "mla":
            lo = _lane_iota((tq, LANE)) < MLA_V

            def head_out(hd):
                a = acc_ref[hd]
                return a / jnp.broadcast_to(a[:, MLA_V:MLA_V + 1], a.shape)

            for jp in range(MLA_HEADS // 2):
                pair = jnp.where(lo, head_out(2 * jp), pltpu.roll(head_out(2 * jp + 1), MLA_V, axis=1))
                o_ref[:, jp * LANE:(jp + 1) * LANE] = pair.astype(BF16)
        else:
            lv = lamv_ref[...]
            lam = (jnp.exp(jnp.sum(lv[0:1, :] * lv[1:2, :], axis=-1, keepdims=True))
                   - jnp.exp(jnp.sum(lv[2:3, :] * lv[3:4, :], axis=-1, keepdims=True)) + lam_init)
            for hd in range(DIFF_HEADS):
                o = acc_ref[2 * hd] / l_ref[2 * hd] - lam * (acc_ref[2 * hd + 1] / l_ref[2 * hd + 1])
                o_ref[:, hd * LANE:(hd + 1) * LANE] = (_rms(o, g_ref[...]) * (1.0 - lam_init)).astype(BF16)


def _attention(kind, q, k, v, n, seq_len, key_len, extra=None, lam_init=None):
    tq = min(512, seq_len)
    tk = min(512, seq_len)
    nq, nk = seq_len // tq, key_len // tk
    wq, wk, wv = q.shape[1], k.shape[1], v.shape[1]
    qmap = lambda b, qi, ki: (b * nq + qi, 0)
    kmap = lambda b, qi, ki: (b * nk + ki, 0)
    in_specs = [pl.BlockSpec((tq, wq), qmap), pl.BlockSpec((tk, wk), kmap), pl.BlockSpec((tk, wv), kmap)]
    args = [q, k, v]
    stat = pltpu.VMEM((ATTN_HEADS, tq, LANE), F32)
    scratch = [stat, stat, pltpu.VMEM((ATTN_HEADS, tq, tk), F32), pltpu.VMEM((ATTN_HEADS, tq, tk), BF16), stat]
    if kind == "mla":
        w_out = MLA_HEADS * MLA_V
    else:
        w_out = DIFF_WIDTH
        scratch.append(stat)
        lamv, g = extra
        in_specs += [pl.BlockSpec(lamv.shape, lambda b, qi, ki: (0, 0)),
                     pl.BlockSpec(g.shape, lambda b, qi, ki: (0, 0))]
        args += [lamv, g]
    return pl.pallas_call(
        functools.partial(_attn_kernel, kind, lam_init),
        grid=(n, nq, nk),
        in_specs=in_specs,
        out_specs=pl.BlockSpec((tq, w_out), qmap),
        out_shape=jax.ShapeDtypeStruct((n * seq_len, w_out), BF16),
        scratch_shapes=scratch,
        compiler_params=_cparams(("arbitrary", "arbitrary", "arbitrary")),
        name=kind + ("_lat" if key_len > seq_len else "_ctx"),
    )(*args)


def _outproj_kernel(ys_ref, om_ref, od_ref, x_ref, w_ref, g1_ref, n2_ref, sc_ref, sh_ref, wr_ref,
                    x1_ref, h2_ref, sco_ref):
    mix = (_dot(ys_ref[...], w_ref[0:512, :]) + _dot(om_ref[...], w_ref[512:1024, :])
           + _dot(od_ref[...], w_ref[1024:1536, :]))
    x1 = x_ref[...] + g1_ref[...] * mix
    x1_ref[...] = x1
    h2 = _rms(x1, n2_ref[...]) * (1.0 + sc_ref[...]) + sh_ref[...]
    h2_ref[...] = h2
    sco_ref[...] = 1.0 / (1.0 + jnp.exp(-_dot(h2.astype(BF16), wr_ref[...])))


def _outproj(ys, om, od, x2d, g1, sc2, sh2, lw, seq_len):
    t_tot, d = x2d.shape
    tm = min(512, seq_len)
    tpb = seq_len // tm
    nb = g1.shape[0]
    bidx = (lambda i: (i // tpb, 0, 0)) if nb > 1 else (lambda i: (0, 0, 0))
    row = lambda i: (i, 0)
    const = lambda i: (0, 0)
    mod = pl.BlockSpec((None, 1, d), bidx)
    return pl.pallas_call(
        _outproj_kernel,
        grid=(t_tot // tm,),
        in_specs=[pl.BlockSpec((tm, 512), row)] * 3 + [
            pl.BlockSpec((tm, d), row), pl.BlockSpec((1536, d), const), mod,
            pl.BlockSpec((1, d), const), mod, mod, pl.BlockSpec((d, LANE), const)],
        out_specs=[pl.BlockSpec((tm, d), row), pl.BlockSpec((tm, d), row), pl.BlockSpec((tm, LANE), row)],
        out_shape=[jax.ShapeDtypeStruct((t_tot, d), F32), jax.ShapeDtypeStruct((t_tot, d), F32),
                   jax.ShapeDtypeStruct((t_tot, LANE), F32)],
        compiler_params=_cparams(("arbitrary",)),
        name="outproj",
    )(ys, om, od, x2d, lw["w_out"], g1, lw["norm2_g"], sc2, sh2, lw["w_router"])


def _route_kernel(sco_ref, bias_ref, idx_ref, w_ref, rank_ref, cnt_ref, run_s):
    i = pl.program_id(0)
    tm = sco_ref.shape[0]
    gsz = N_EXPERTS // N_EXPERT_GROUPS

    @pl.when(i == 0)
    def _():
        run_s[...] = jnp.zeros(run_s.shape, F32)

    shape3 = (N_EXPERT_GROUPS, gsz, tm)
    sco = sco_ref[...]
    scores3 = sco.T[0:N_EXPERTS, :].reshape(shape3)
    sel3 = (sco + bias_ref[...]).T[0:N_EXPERTS, :].reshape(shape3)
    sub = lax.broadcasted_iota(jnp.int32, shape3, 1)
    gid = lax.broadcasted_iota(jnp.int32, shape3, 0)
    eid = gid * gsz + sub

    def max_all(a):
        return jnp.max(jnp.max(a, axis=0, keepdims=True), axis=1, keepdims=True)

    def min_all(a):
        return jnp.min(jnp.min(a, axis=0, keepdims=True), axis=1, keepdims=True)

    def sum_all(a):
        return jnp.sum(jnp.sum(a, axis=0, keepdims=True), axis=1, keepdims=True)

    m1 = jnp.max(sel3, axis=1, keepdims=True)
    first = jnp.min(jnp.where(sel3 == m1, sub, gsz), axis=1, keepdims=True)
    m2 = jnp.max(jnp.where(sub == first, NEG_INF, sel3), axis=1, keepdims=True)
    gscore = m1 + m2
    gid1 = lax.broadcasted_iota(jnp.int32, gscore.shape, 0)
    gsel = jnp.zeros(gscore.shape, F32)
    for _ in range(TOPK_GROUPS):
        gm = jnp.max(gscore, axis=0, keepdims=True)
        gfirst = jnp.min(jnp.where(gscore == gm, gid1, N_EXPERT_GROUPS), axis=0, keepdims=True)
        pick = gid1 == gfirst
        gsel = jnp.where(pick, 1.0, gsel)
        gscore = jnp.where(pick, NEG_INF, gscore)
    selm = jnp.where(gsel > 0.5, sel3, NEG_INF)
    chosen = jnp.zeros(shape3, F32)
    idxs, ws = [], []
    for _ in range(TOP_K):
        mx = max_all(selm)
        efirst = min_all(jnp.where(selm == mx, eid, N_EXPERTS))
        pick = eid == efirst
        chosen = jnp.where(pick, 1.0, chosen)
        idxs.append(efirst)
        ws.append(sum_all(jnp.where(pick, scores3, 0.0)))
        selm = jnp.where(pick, NEG_INF, selm)
    wsum = ws[0]
    for k in range(1, TOP_K):
        wsum = wsum + ws[k]
    r = lax.broadcasted_iota(jnp.int32, (tm, tm), 0)
    c = lax.broadcasted_iota(jnp.int32, (tm, tm), 1)
    before = jnp.where(r < c, 1.0, 0.0).astype(BF16)
    cm = chosen.reshape(N_EXPERTS, tm)
    run = run_s[...]
    excl3 = (_dot(cm.astype(BF16), before) + run[:, 0:1]).reshape(shape3)
    run_s[...] = run + jnp.sum(cm, axis=1, keepdims=True)
    for k in range(TOP_K):
        idx_ref[k:k + 1, :] = idxs[k].reshape(1, tm)
        w_ref[k:k + 1, :] = (ws[k] / wsum * ROUTED_SCALE).reshape(1, tm)
        rk = sum_all(jnp.where(eid == idxs[k], excl3, 0.0))
        rank_ref[k:k + 1, :] = rk.reshape(1, tm).astype(jnp.int32)
    cnt_ref[...] = run_s[...]


def _route(scores, bias_row):
    t_tot = scores.shape[0]
    tm = 512
    col = lambda i: (0, i)
    return pl.pallas_call(
        _route_kernel,
        grid=(t_tot // tm,),
        in_specs=[pl.BlockSpec((tm, LANE), lambda i: (i, 0)), pl.BlockSpec((1, LANE), lambda i: (0, 0))],
        out_specs=[pl.BlockSpec((TOP_K, tm), col), pl.BlockSpec((TOP_K, tm), col),
                   pl.BlockSpec((TOP_K, tm), col), pl.BlockSpec((N_EXPERTS, LANE), lambda i: (0, 0))],
        out_shape=[jax.ShapeDtypeStruct((TOP_K, t_tot), jnp.int32), jax.ShapeDtypeStruct((TOP_K, t_tot), F32),
                   jax.ShapeDtypeStruct((TOP_K, t_tot), jnp.int32),
                   jax.ShapeDtypeStruct((N_EXPERTS, LANE), F32)],
        scratch_shapes=[pltpu.VMEM((N_EXPERTS, LANE), F32)],
        compiler_params=_cparams(("arbitrary",)),
        name="route",
    )(scores, bias_row)


def _slot(ps_ref, idx_ref, rank_ref, k, t):
    return ps_ref[idx_ref[k, t]] + rank_ref[k, t]


def _dispatch_kernel(ps_ref, h_ref, idx_ref, rank_ref, _, xs_ref, sem):
    tm = h_ref.shape[0]

    def body(t, carry):
        for k in range(TOP_K):
            dst = xs_ref.at[pl.ds(_slot(ps_ref, idx_ref, rank_ref, k, t), 1), :]
            pltpu.make_async_copy(h_ref.at[pl.ds(t, 1), :], dst, sem).start()
        return carry

    lax.fori_loop(0, tm, body, 0)
    for k in range(TOP_K):
        pltpu.make_async_copy(h_ref, xs_ref.at[pl.ds(0, tm), :], sem).wait()


def _dispatch(pad_start, h2, idx_t, rank_t, slots):
    t_tot, d = h2.shape
    tm = 256
    smem = lambda: pl.BlockSpec((TOP_K, tm), lambda i, ps: (0, i), memory_space=pltpu.SMEM)
    xs0 = jnp.zeros((slots, d), F32)
    return pl.pallas_call(
        _dispatch_kernel,
        grid_spec=pltpu.PrefetchScalarGridSpec(
            num_scalar_prefetch=1,
            grid=(t_tot // tm,),
            in_specs=[pl.BlockSpec((tm, d), lambda i, ps: (i, 0)), smem(), smem(),
                      pl.BlockSpec(memory_space=pl.ANY)],
            out_specs=pl.BlockSpec(memory_space=pl.ANY),
            scratch_shapes=[pltpu.SemaphoreType.DMA(())]),
        out_shape=jax.ShapeDtypeStruct((slots, d), F32),
        input_output_aliases={4: 0},
        compiler_params=_cparams(("arbitrary",)),
        name="dispatch",
    )(pad_start, h2, idx_t, rank_t, xs0)


def _experts_kernel(bexp_ref, nused_ref, x_ref, wg_ref, wu_ref, wd_ref, y_ref, wg_b, wu_b, wd_b):
    i = pl.program_id(0)
    used = i < nused_ref[0]
    fresh = (i == 0) | (bexp_ref[i] != bexp_ref[jnp.maximum(i - 1, 0)])

    @pl.when(used & fresh)
    def _():
        wg_b[...] = wg_ref[...].astype(BF16)
        wu_b[...] = wu_ref[...].astype(BF16)
        wd_b[...] = wd_ref[...].astype(BF16)

    @pl.when(used)
    def _():
        x = x_ref[...].astype(BF16)
        hmid = _silu(_dot(x, wg_b[...])) * _dot(x, wu_b[...])
        y_ref[...] = _dot(hmid.astype(BF16), wd_b[...])

    @pl.when(jnp.logical_not(used))
    def _():
        y_ref[...] = jnp.zeros(y_ref.shape, F32)


def _experts(x_sorted, block_exp, n_used, wg, wu, wd, layer):
    slots, d = x_sorted.shape
    n_blocks = slots // EXPERT_ROWS
    hid = wg.shape[-1]
    wmap = lambda i, be, nu: (layer, be[i], 0, 0)
    return pl.pallas_call(
        _experts_kernel,
        grid_spec=pltpu.PrefetchScalarGridSpec(
            num_scalar_prefetch=2,
            grid=(n_blocks,),
            in_specs=[pl.BlockSpec((EXPERT_ROWS, d), lambda i, be, nu: (i, 0)),
                      pl.BlockSpec((None, None, d, hid), wmap),
                      pl.BlockSpec((None, None, d, hid), wmap),
                      pl.BlockSpec((None, None, hid, d), wmap)],
            out_specs=pl.BlockSpec((EXPERT_ROWS, d), lambda i, be, nu: (i, 0)),
            scratch_shapes=[pltpu.VMEM((d, hid), BF16), pltpu.VMEM((d, hid), BF16), pltpu.VMEM((hid, d), BF16)]),
        out_shape=jax.ShapeDtypeStruct((slots, d), F32),
        compiler_params=_cparams(("arbitrary",)),
        name="experts",
    )(block_exp, n_used, x_sorted, wg, wu, wd)


def _combine_kernel(final, ps_ref, h2_ref, idx_ref, rank_ref, w_ref, x1_ref, g2_ref, wsg_ref, wsu_ref, wsd_ref,
                    fg_ref, ys_ref, o_ref, yg_s, sem):
    tm = h2_ref.shape[0]

    def body(t, carry):
        for k in range(TOP_K):
            src = ys_ref.at[pl.ds(_slot(ps_ref, idx_ref, rank_ref, k, t), 1), :]
            pltpu.make_async_copy(src, yg_s.at[k, pl.ds(t, 1), :], sem).start()
        return carry

    lax.fori_loop(0, tm, body, 0)
    h2 = h2_ref[...].astype(BF16)
    shared = _dot((_silu(_dot(h2, wsg_ref[...])) * _dot(h2, wsu_ref[...])).astype(BF16), wsd_ref[...])
    for k in range(TOP_K):
        pltpu.make_async_copy(ys_ref.at[pl.ds(0, tm), :], yg_s.at[k], sem).wait()
    w = w_ref[...]
    routed = yg_s[0] * w[:, 0:1]
    for k in range(1, TOP_K):
        routed = routed + yg_s[k] * w[:, k:k + 1]
    x2 = x1_ref[...] + g2_ref[...] * (routed + shared)
    o_ref[...] = _rms(x2, fg_ref[...]) if final else x2


def _combine(pad_start, h2, idx_t, rank_t, w_tok, y_sorted, x1, g2, lw, final_g, seq_len, final):
    t_tot, d = x1.shape
    tm = min(256, seq_len)
    tpb = seq_len // tm
    nb = g2.shape[0]
    bidx = (lambda i, ps: (i // tpb, 0, 0)) if nb > 1 else (lambda i, ps: (0, 0, 0))
    row = lambda i, ps: (i, 0)
    const = lambda i, ps: (0, 0)
    smem = lambda: pl.BlockSpec((TOP_K, tm), lambda i, ps: (0, i), memory_space=pltpu.SMEM)
    hid = lw["w_s_gate"].shape[1]
    return pl.pallas_call(
        functools.partial(_combine_kernel, final),
        grid_spec=pltpu.PrefetchScalarGridSpec(
            num_scalar_prefetch=1,
            grid=(t_tot // tm,),
            in_specs=[pl.BlockSpec((tm, d), row), smem(), smem(), pl.BlockSpec((tm, TOP_K), row),
                      pl.BlockSpec((tm, d), row), pl.BlockSpec((None, 1, d), bidx),
                      pl.BlockSpec((d, hid), const), pl.BlockSpec((d, hid), const), pl.BlockSpec((hid, d), const),
                      pl.BlockSpec((1, d), const), pl.BlockSpec(memory_space=pl.ANY)],
            out_specs=pl.BlockSpec((tm, d), row),
            scratch_shapes=[pltpu.VMEM((TOP_K, tm, d), F32), pltpu.SemaphoreType.DMA(())]),
        out_shape=jax.ShapeDtypeStruct((t_tot, d), F32),
        compiler_params=_cparams(("arbitrary",)),
        name="combine_final" if final else "combine",
    )(pad_start, h2, idx_t, rank_t, w_tok, x1, g2, lw["w_s_gate"], lw["w_s_up"], lw["w_s_down"], final_g, y_sorted)


def _rope_tables(seq_len):
    t = jnp.arange(seq_len, dtype=jnp.int32)
    rows = (t // GRID_W).astype(F32)[:, None]
    cols = (t % GRID_W).astype(F32)[:, None]
    lane = jnp.arange(LANE)

    def build(rel, width, active):
        dr = width // 2
        half = dr // 2
        within = rel % dr
        freq = ROPE_BASE ** (-(within % half).astype(F32) / half)
        pos = jnp.where((rel < dr)[None, :], rows, cols)
        ang = pos * freq[None, :]
        sign = jnp.where(within < half, -1.0, 1.0)
        cos = jnp.where(active[None, :], jnp.cos(ang), 1.0)
        sin = jnp.where(active[None, :], jnp.sin(ang) * sign[None, :], 0.0)
        return cos.astype(F32), sin.astype(F32)

    act_q = (lane >= _KROPE_LANE) & (lane < _KROPE_LANE + MLA_ROPE_DIM)
    cq, sq = build(jnp.clip(lane - _KROPE_LANE, 0, MLA_ROPE_DIM - 1), MLA_ROPE_DIM, act_q)
    cd, sd = build(lane % DIFF_HEAD_DIM, DIFF_HEAD_DIM, jnp.ones((LANE,), bool))
    return cq, sq, cd, sd


def _prep_layer(p, l):
    w_in = p["w_in"][l]
    d = w_in.shape[0]
    z0, x0, t0, q0, kv0, dq0, dk0, dv0 = 0, 512, 1536, 1552, 1808, 1968, 2480, 2992
    misc = jnp.zeros((d, LANE), F32)
    misc = misc.at[:, 0:2 * SSD_HEADS].set(w_in[:, t0:q0])
    misc = misc.at[:, _KROPE_LANE:_KROPE_LANE + MLA_ROPE_DIM].set(w_in[:, kv0 + MLA_KV_RANK:dq0])
    w_cat = jnp.concatenate([w_in[:, z0:x0], w_in[:, x0:t0], misc, w_in[:, q0:kv0],
                             w_in[:, kv0:kv0 + MLA_KV_RANK], w_in[:, dq0:dk0], w_in[:, dk0:dv0],
                             w_in[:, dv0:]], axis=1).astype(BF16)
    qd = MLA_NOPE + MLA_ROPE_DIM
    w_qb = jnp.pad(p["w_q_b"][l].reshape(MLA_Q_RANK, MLA_HEADS, qd), ((0, 0), (0, 0), (0, LANE - qd)))
    w_kvb = p["w_kv_b"][l].reshape(MLA_KV_RANK, MLA_HEADS, MLA_NOPE + MLA_V)
    w_k = jnp.pad(w_kvb[:, :, :MLA_NOPE], ((0, 0), (0, 0), (0, LANE - MLA_NOPE)))
    vv = w_kvb[:, :, MLA_NOPE:]
    w_v = jnp.pad(vv, ((0, 0), (0, 0), (0, LANE - MLA_V)))
    pad_lane = lambda a: jnp.pad(a, ((0, 0), (0, LANE - a.shape[1])))
    lamv = jnp.stack([p["lambda_q1"][l], p["lambda_k1"][l], p["lambda_q2"][l], p["lambda_k2"][l]])
    return {
        "w_in": w_cat,
        "w_qb": w_qb.reshape(MLA_Q_RANK, MLA_HEADS * LANE).astype(BF16),
        "w_k": w_k.reshape(MLA_KV_RANK, MLA_HEADS * LANE).astype(BF16),
        "w_v": w_v.reshape(MLA_KV_RANK, MLA_HEADS * LANE).astype(BF16),
        "gqa": p["q_a_norm_g"][l][None, :], "gkv": p["kv_a_norm_g"][l][None, :],
        "norm1_g": p["norm1_g"][l][None, :], "norm2_g": p["norm2_g"][l][None, :],
        "conv_w": p["conv_w"][l], "conv_b": p["conv_b"][l][None, :],
        "dt_bias": pad_lane(p["dt_bias"][l].reshape(1, -1)), "a_log": pad_lane(p["a_log"][l].reshape(1, -1)),
        "d_skip": jnp.repeat(p["d_skip"][l], SSD_HEAD_DIM)[None, :], "ssd_norm_g": p["ssd_norm_g"][l][None, :],
        "lamv": pad_lane(lamv), "subln_g": p["diff_subln_g"][l][None, :],
        "w_out": p["w_out"][l].astype(BF16),
        "w_router": pad_lane(p["w_router"][l]).astype(BF16),
        "router_bias": pad_lane(p["router_bias"][l][None, :]),
        "w_s_gate": p["w_s_gate"][l].astype(BF16), "w_s_up": p["w_s_up"][l].astype(BF16),
        "w_s_down": p["w_s_down"][l].astype(BF16),
    }


def _lambda_init(layer_idx):
    return 0.8 - 0.6 * math.exp(-0.3 * layer_idx)


def _moe(h2, scores, x1, g2, lw, ew, layer, final_g, seq_len, final):
    t_tot, d = h2.shape
    idx_t, w_t, rank_t, cnt = _route(scores, lw["router_bias"])
    counts = cnt[:, 0].astype(jnp.int32)
    padded = (counts + EXPERT_ROWS - 1) // EXPERT_ROWS * EXPERT_ROWS
    pad_end = jnp.cumsum(padded)
    pad_start = pad_end - padded
    n_blocks = t_tot * TOP_K // EXPERT_ROWS + N_EXPERTS
    block_start = jnp.arange(n_blocks, dtype=jnp.int32) * EXPERT_ROWS
    block_exp = jnp.minimum(jnp.sum((pad_end[None, :] <= block_start[:, None]).astype(jnp.int32), axis=1),
                            N_EXPERTS - 1)
    n_used = pad_end[-1:] // EXPERT_ROWS
    x_sorted = _dispatch(pad_start, h2, idx_t, rank_t, n_blocks * EXPERT_ROWS)
    y_sorted = _experts(x_sorted, block_exp, n_used, ew[0], ew[1], ew[2], layer)
    return _combine(pad_start, h2, idx_t, rank_t, w_t.T, y_sorted, x1, g2, lw, final_g, seq_len, final)


def _block(x2d, mod, lw, ew, layer_idx, n, seq_len, tables, cached, final_g, final):
    sh1, sc1, g1, sh2, sc2, g2 = [mod[:, i:i + 1, :] for i in range(6)]
    latent = cached is not None
    lam_init = _lambda_init(layer_idx)
    if latent:
        past = cached["ckv"].shape[1]
        kr_pad = jnp.pad(cached["krope"].reshape(n * past, MLA_ROPE_DIM),
                         ((0, 0), (_KROPE_LANE, LANE - _KROPE_LANE - MLA_ROPE_DIM)))
        kv_bufs = _ctxkv(cached["ckv"].reshape(n * past, MLA_KV_RANK), kr_pad,
                         cached["dk"].reshape(n * past, DIFF_WIDTH), cached["dv"].reshape(n * past, DIFF_WIDTH),
                         lw, n, seq_len, past)
        res = _inproj(x2d, sc1, sh1, lw["norm1_g"], lw, seq_len, tables, kv_bufs, past)
        init, want_state, key_len = cached["ssd"], False, seq_len + past
    else:
        res = _inproj(x2d, sc1, sh1, lw["norm1_g"], lw, seq_len, None, None, 0)
        init, want_state, key_len = None, True, seq_len
    z, xbc, misc, q, dq, k, v, dk, dv = res[:9]
    y_ssd, st = _ssd(xbc, misc, z, lw, n, seq_len, init, want_state)
    o_mla = _attention("mla", q, k, v, n, seq_len, key_len)
    o_diff = _attention("diff", dq, dk, dv, n, seq_len, key_len, (lw["lamv"], lw["subln_g"]), lam_init)
    ctx_new = None
    if not latent:
        ckv_f, dk_f, dv_f = res[9:]
        ctx_new = (st.reshape(n, 2, SSD_HEADS, SSD_HEAD_DIM, SSD_STATE),
                   ckv_f.reshape(n, seq_len, MLA_KV_RANK),
                   misc[:, _KROPE_LANE:_KROPE_LANE + MLA_ROPE_DIM].reshape(n, seq_len, MLA_ROPE_DIM),
                   dk_f.reshape(n, seq_len, DIFF_HEADS, 2 * DIFF_HEAD_DIM),
                   dv_f.reshape(n, seq_len, DIFF_HEADS, 2 * DIFF_HEAD_DIM))
    x1, h2, scores = _outproj(y_ssd, o_mla, o_diff, x2d, g1, sc2, sh2, lw, seq_len)
    x2 = _moe(h2, scores, x1, g2, lw, ew, layer_idx, final_g, seq_len, final)
    return x2, ctx_new


def kernel(x_prompt, x_sample, state_ssd, cache_mla_ckv, cache_mla_krope, cache_diff_k, cache_diff_v, c, c_ctx, w_ada, b_ada, norm1_g, norm2_g, w_in, conv_w, conv_b, dt_bias, a_log, d_skip, ssd_norm_g, q_a_norm_g, w_q_b, kv_a_norm_g, w_kv_b, lambda_q1, lambda_k1, lambda_q2, lambda_k2, diff_subln_g, w_out, w_router, router_bias, w_e_gate, w_e_up, w_e_down, w_s_gate, w_s_up, w_s_down, final_norm_g):
    p = dict(norm1_g=norm1_g, norm2_g=norm2_g, w_in=w_in, conv_w=conv_w, conv_b=conv_b, dt_bias=dt_bias,
             a_log=a_log, d_skip=d_skip, ssd_norm_g=ssd_norm_g, q_a_norm_g=q_a_norm_g, w_q_b=w_q_b,
             kv_a_norm_g=kv_a_norm_g, w_kv_b=w_kv_b, lambda_q1=lambda_q1, lambda_k1=lambda_k1,
             lambda_q2=lambda_q2, lambda_k2=lambda_k2, diff_subln_g=diff_subln_g, w_out=w_out,
             w_router=w_router, router_bias=router_bias, w_s_gate=w_s_gate, w_s_up=w_s_up, w_s_down=w_s_down)
    ew = (w_e_gate, w_e_up, w_e_down)
    depth = w_in.shape[0]
    nc, lc, d = x_prompt.shape
    ns, ls, _ = x_sample.shape
    cond_rows = 16
    cond = jnp.zeros((cond_rows, d), F32).at[0:ns].set(c).at[ns].set(c_ctx)
    mod = _ada(cond, w_ada, b_ada).reshape(depth, cond_rows, 6, d)
    tables = _rope_tables(ls)
    final_g = final_norm_g[None, :]
    xp = x_prompt.reshape(nc * lc, d)
    xs = x_sample.reshape(ns * ls, d)
    news = []
    for l in range(depth):
        lw = _prep_layer(p, l)
        final = l == depth - 1
        xp, ctx_new = _block(xp, mod[l, ns:ns + 1], lw, ew, l, nc, lc, None, None, final_g, final)
        news.append(ctx_new)
        cached = {"ssd": state_ssd[:, l].reshape(ns, 2, SSD_GROUPS, 4 * SSD_HEAD_DIM, SSD_STATE),
                  "ckv": cache_mla_ckv[:, l], "krope": cache_mla_krope[:, l],
                  "dk": cache_diff_k[:, l], "dv": cache_diff_v[:, l]}
        xs, _ = _block(xs, mod[l, 0:ns], lw, ew, l, ns, ls, tables, cached, final_g, final)
    stack = lambda i: jnp.stack([nw[i] for nw in news], axis=1)
    return (xp.reshape(nc, lc, d), xs.reshape(ns, ls, d), stack(0), stack(1), stack(2), stack(3), stack(4))
```

```python
import functools
import math

import jax
import jax.numpy as jnp
from jax import lax
from jax.experimental import pallas as pl
from jax.experimental.pallas import tpu as pltpu

F32 = jnp.float32
BF16 = jnp.bfloat16

GRID_W = 64
ROPE_BASE = 10000.0
EPS = 1e-6
SSD_HEADS = 8
SSD_HEAD_DIM = 64
SSD_WIDTH = 512
SSD_GROUPS = 2
SSD_STATE = 128
SSD_CHUNK = 128
SSD_CONV_CH = 1024
MLA_HEADS = 8
MLA_Q_RANK = 256
MLA_KV_RANK = 128
MLA_NOPE = 64
MLA_ROPE_DIM = 32
MLA_V = 64
MLA_SCALE = (MLA_NOPE + MLA_ROPE_DIM) ** -0.5
DIFF_HEADS = 4
DIFF_HEAD_DIM = 64
DIFF_WIDTH = 512
DIFF_SCALE = DIFF_HEAD_DIM ** -0.5
N_EXPERTS = 64
N_EXPERT_GROUPS = 8
TOPK_GROUPS = 4
TOP_K = 8
EXPERT_HIDDEN = 256
ROUTED_SCALE = 2.5

LANE = 128
VMEM_LIMIT = 56 * 1024 * 1024
EXPERT_ROWS = 512
DMA_UNROLL = 4
ATTN_HEADS = 8
ATTN_STRIP = 16
NEG_INF = float("-inf")
LOG2E = math.log2(math.e)


def _cparams(sem, **kw):
    return pltpu.CompilerParams(dimension_semantics=sem, vmem_limit_bytes=VMEM_LIMIT, **kw)


def _silu(x):
    return x * (1.0 / (1.0 + jnp.exp(-x)))


def _softplus(x):
    return jnp.maximum(x, 0.0) + jnp.log1p(jnp.exp(-jnp.abs(x)))


def _rms(x, g):
    return x * lax.rsqrt(jnp.mean(x * x, axis=-1, keepdims=True) + EPS) * g


def _dot(a, b):
    return jnp.dot(a, b, preferred_element_type=F32)


def _dot_nt(a, b):
    return lax.dot_general(a, b, (((1,), (1,)), ((), ())), preferred_element_type=F32)


def _lane_iota(shape):
    return lax.broadcasted_iota(jnp.int32, shape, len(shape) - 1)


def _with_ones_lane(v):
    return jnp.where((_lane_iota(v.shape) & (LANE - 1)) == MLA_V, 1.0, v)


def _rot_half(x, half):
    lane = _lane_iota(x.shape)
    up = pltpu.roll(x, LANE - half, axis=1)
    dn = pltpu.roll(x, half, axis=1)
    return jnp.where((lane & (2 * half - 1)) < half, up, dn)


def _ada_kernel(c_ref, w_ref, b_ref, o_ref):
    s = _silu(c_ref[...])
    o_ref[...] = _dot(s.astype(BF16), w_ref[...].astype(BF16)) + b_ref[...]


def _ada(cond, w_ada, b_ada):
    depth, d, n6 = w_ada.shape
    rows = cond.shape[0]
    tn = 1536
    return pl.pallas_call(
        _ada_kernel,
        grid=(depth, n6 // tn),
        in_specs=[pl.BlockSpec((rows, d), lambda l, j: (0, 0)),
                  pl.BlockSpec((None, d, tn), lambda l, j: (l, 0, j)),
                  pl.BlockSpec((None, 1, tn), lambda l, j: (l, 0, j))],
        out_specs=pl.BlockSpec((None, rows, tn), lambda l, j: (l, 0, j)),
        out_shape=jax.ShapeDtypeStruct((depth, rows, n6), F32),
        compiler_params=_cparams(("arbitrary", "arbitrary")),
        name="ada",
    )(cond, w_ada, b_ada.reshape(depth, 1, n6))


_C_Z, _C_XBC, _C_MISC, _C_QA, _C_CKV, _C_DQ, _C_DK, _C_DV, _C_END = (
    0, 512, 1536, 1664, 1920, 2048, 2560, 3072, 3584)
_KROPE_LANE = 64


def _inproj_kernel(latent, x_ref, sc_ref, sh_ref, g1_ref, w_ref, gqa_ref, gkv_ref, wqb_ref, wk_ref, wv_ref,
                   *rest):
    if latent:
        (cq_ref, sq_ref, cd_ref, sd_ref, _, _, _, _,
         z_ref, xbc_ref, misc_ref, q_ref, dq_ref, k_ref, v_ref, dk_ref, dv_ref) = rest
    else:
        (z_ref, xbc_ref, misc_ref, q_ref, dq_ref, k_ref, v_ref, dk_ref, dv_ref,
         ckv_ref, dkf_ref, dvf_ref) = rest
    x = x_ref[...]
    h = (_rms(x, g1_ref[...]) * (1.0 + sc_ref[...]) + sh_ref[...]).astype(BF16)

    def seg(a, b):
        return _dot(h, w_ref[:, a:b])

    z_ref[...] = seg(_C_Z, _C_XBC).astype(BF16)
    xbc_ref[...] = seg(_C_XBC, _C_MISC).astype(BF16)
    misc = seg(_C_MISC, _C_QA)
    misc_ref[...] = misc

    qn = _rms(seg(_C_QA, _C_CKV), gqa_ref[...]).astype(BF16)
    q = _dot(qn, wqb_ref[...]) * (MLA_SCALE * LOG2E)
    ckv = _rms(seg(_C_CKV, _C_DQ), gkv_ref[...])
    ckv_b = ckv.astype(BF16)
    kn = _dot(ckv_b, wk_ref[...])
    v_ref[...] = _with_ones_lane(_dot(ckv_b, wv_ref[...])).astype(BF16)
    lane = _lane_iota(misc.shape)
    in_rope = (lane >= _KROPE_LANE) & (lane < _KROPE_LANE + MLA_ROPE_DIM)
    if latent:
        cq, sq = cq_ref[...], sq_ref[...]
        kr = jnp.where(in_rope, misc * cq + _rot_half(misc, MLA_ROPE_DIM // 4) * sq, 0.0)
    else:
        kr = jnp.where(in_rope, misc, 0.0)
    for hd in range(MLA_HEADS):
        sl = slice(hd * LANE, (hd + 1) * LANE)
        qh = q[:, sl]
        if latent:
            qh = qh * cq + _rot_half(qh, MLA_ROPE_DIM // 4) * sq
        q_ref[:, sl] = qh.astype(BF16)
        k_ref[:, sl] = (kn[:, sl] + kr).astype(BF16)

    dq = seg(_C_DQ, _C_DK) * (DIFF_SCALE * LOG2E)
    dk = seg(_C_DK, _C_DV)
    dv = seg(_C_DV, _C_END)
    dv_ref[...] = dv.astype(BF16)
    lo = lane < DIFF_HEAD_DIM
    if latent:
        cd, sd = cd_ref[...], sd_ref[...]
    for hd in range(DIFF_HEADS):
        sl = slice(hd * LANE, (hd + 1) * LANE)
        a = dq[:, sl]
        b = dk[:, sl]
        if latent:
            a = a * cd + _rot_half(a, DIFF_HEAD_DIM // 4) * sd
            b = b * cd + _rot_half(b, DIFF_HEAD_DIM // 4) * sd
        dq_ref[:, 2 * hd * LANE:(2 * hd + 1) * LANE] = jnp.where(lo, a, 0.0).astype(BF16)
        dq_ref[:, (2 * hd + 1) * LANE:(2 * hd + 2) * LANE] = jnp.where(lo, 0.0, a).astype(BF16)
        dk_ref[:, sl] = b.astype(BF16)
    if not latent:
        ckv_ref[...] = ckv
        dkf_ref[...] = dk
        dvf_ref[...] = dv


def _inproj(x2d, sc, sh, g1, lw, seq_len, tables, kv_bufs, past_len):
    t_tot, d = x2d.shape
    latent = tables is not None
    tm = min(512, seq_len)
    tpb = seq_len // tm
    ppb = past_len // tm if latent else 0
    nb = sc.shape[0]
    bidx = (lambda i: (i // tpb, 0, 0)) if nb > 1 else (lambda i: (0, 0, 0))
    row = lambda i: (i, 0)
    kvrow = lambda i: ((i // tpb) * (tpb + ppb) + i % tpb, 0)
    const = lambda i: (0, 0)
    in_specs = [pl.BlockSpec((tm, d), row),
                pl.BlockSpec((None, 1, d), bidx), pl.BlockSpec((None, 1, d), bidx),
                pl.BlockSpec((1, d), const),
                pl.BlockSpec((d, _C_END), const),
                pl.BlockSpec((1, MLA_Q_RANK), const), pl.BlockSpec((1, MLA_KV_RANK), const),
                pl.BlockSpec((MLA_Q_RANK, MLA_HEADS * LANE), const),
                pl.BlockSpec((MLA_KV_RANK, MLA_HEADS * LANE), const),
                pl.BlockSpec((MLA_KV_RANK, MLA_HEADS * LANE), const)]
    args = [x2d, sc, sh, g1, lw["w_in"], lw["gqa"], lw["gkv"], lw["w_qb"], lw["w_k"], lw["w_v"]]
    aliases = {}
    if latent:
        tab = lambda i: (i % tpb, 0)
        in_specs += [pl.BlockSpec((tm, LANE), tab)] * 4
        args += list(tables)
        for j, buf in enumerate(kv_bufs):
            aliases[len(args)] = 5 + j
            in_specs.append(pl.BlockSpec(memory_space=pl.ANY))
            args.append(buf)
    kv_rows = (t_tot // seq_len) * (seq_len + past_len) if latent else t_tot

    def o(width, dtype, kv=False):
        return (pl.BlockSpec((tm, width), kvrow if kv else row),
                jax.ShapeDtypeStruct((kv_rows if kv else t_tot, width), dtype))

    outs = [o(512, BF16), o(1024, BF16), o(LANE, F32), o(1024, BF16), o(1024, BF16),
            o(1024, BF16, True), o(1024, BF16, True), o(512, BF16, True), o(512, BF16, True)]
    if not latent:
        outs += [o(MLA_KV_RANK, F32), o(512, F32), o(512, F32)]
    return pl.pallas_call(
        functools.partial(_inproj_kernel, latent),
        grid=(t_tot // tm,),
        in_specs=in_specs,
        out_specs=[s for s, _ in outs],
        out_shape=[s for _, s in outs],
        input_output_aliases=aliases,
        compiler_params=_cparams(("arbitrary",)),
        name="inproj_lat" if latent else "inproj_ctx",
    )(*args)


def _ctxkv_kernel(ckv_ref, kr_ref, dkc_ref, dvc_ref, wk_ref, wv_ref, _k0, _v0, _dk0, _dv0,
                  k_ref, v_ref, dk_ref, dv_ref):
    c = ckv_ref[...].astype(BF16)
    kn = _dot(c, wk_ref[...])
    kr = kr_ref[...]
    for hd in range(MLA_HEADS):
        sl = slice(hd * LANE, (hd + 1) * LANE)
        k_ref[:, sl] = (kn[:, sl] + kr).astype(BF16)
    v_ref[...] = _with_ones_lane(_dot(c, wv_ref[...])).astype(BF16)
    dk_ref[...] = dkc_ref[...].astype(BF16)
    dv_ref[...] = dvc_ref[...].astype(BF16)


def _ctxkv(ckv2d, kr_pad, dkc, dvc, lw, n, seq_len, past_len):
    tm = min(512, seq_len)
    tpb, ppb = seq_len // tm, past_len // tm
    row = lambda i: (i, 0)
    orow = lambda i: ((i // ppb) * (tpb + ppb) + tpb + i % ppb, 0)
    const = lambda i: (0, 0)
    w = MLA_HEADS * LANE
    rows = n * (seq_len + past_len)
    shapes = [jax.ShapeDtypeStruct((rows, w), BF16)] * 2 + [jax.ShapeDtypeStruct((rows, DIFF_WIDTH), BF16)] * 2
    blanks = [jnp.zeros(s.shape, s.dtype) for s in shapes]
    return pl.pallas_call(
        _ctxkv_kernel,
        grid=(n * ppb,),
        in_specs=[pl.BlockSpec((tm, MLA_KV_RANK), row), pl.BlockSpec((tm, LANE), row),
                  pl.BlockSpec((tm, DIFF_WIDTH), row), pl.BlockSpec((tm, DIFF_WIDTH), row),
                  pl.BlockSpec((MLA_KV_RANK, w), const), pl.BlockSpec((MLA_KV_RANK, w), const)]
        + [pl.BlockSpec(memory_space=pl.ANY)] * 4,
        out_specs=[pl.BlockSpec((tm, w), orow), pl.BlockSpec((tm, w), orow),
                   pl.BlockSpec((tm, DIFF_WIDTH), orow), pl.BlockSpec((tm, DIFF_WIDTH), orow)],
        out_shape=shapes,
        input_output_aliases={6: 0, 7: 1, 8: 2, 9: 3},
        compiler_params=_cparams(("arbitrary",)),
        name="ctxkv",
    )(ckv2d, kr_pad, dkc, dvc, lw["w_k"], lw["w_v"], *blanks)


_HALO = 16


def _ssd_kernel(nblk, tb, has_init, want_state, *refs):
    it = iter(refs)
    xbc_ref, prev_ref, next_ref, misc_ref, z_ref = [next(it) for _ in range(5)]
    cw_ref, cb_ref, dtb_ref, alog_ref, dsk_ref, gn_ref = [next(it) for _ in range(6)]
    init_ref = next(it) if has_init else None
    y_ref = next(it)
    st_ref = next(it) if want_state else None
    yf_s, s_s, xc_s, dt_s, da_s = [next(it) for _ in range(5)]

    j = pl.program_id(1)
    nch = tb // SSD_CHUNK
    blk = jnp.where(j < nblk, j, 2 * nblk - 1 - j)

    xb = xbc_ref[...].astype(F32)
    rowid = lax.broadcasted_iota(jnp.int32, xb.shape, 0)
    prev_row = prev_ref[_HALO - 1:_HALO, :].astype(F32) * (blk > 0).astype(F32)
    next_row = next_ref[0:1, :].astype(F32) * (blk < nblk - 1).astype(F32)
    xm1 = jnp.where(rowid == 0, prev_row, pltpu.roll(xb, 1, axis=0))
    xp1 = jnp.where(rowid == tb - 1, next_row, pltpu.roll(xb, tb - 1, axis=0))
    cw = cw_ref[...]
    conv = xm1 * cw[0:1, :] + xb * cw[1:2, :] + xp1 * cw[2:3, :] + cb_ref[...]
    xc_s[...] = _silu(conv)
    dt = _softplus(misc_ref[...] + dtb_ref[...])
    dt_s[...] = dt
    da_s[...] = dt * (-jnp.exp(alog_ref[...]))

    def load_init(d):
        if has_init:
            s_s[...] = init_ref[d]
        else:
            s_s[...] = jnp.zeros(s_s.shape, F32)

    @pl.when(j == 0)
    def _():
        load_init(0)

    @pl.when(j == nblk)
    def _():
        if want_state:
            st_ref[0] = s_s[...]
        load_init(1)

    ri = lax.broadcasted_iota(jnp.int32, (SSD_CHUNK, SSD_CHUNK), 0)
    ci = lax.broadcasted_iota(jnp.int32, (SSD_CHUNK, SSD_CHUNK), 1)
    lane = ci
    lo = lane < SSD_HEAD_DIM

    def bcol(m, li):
        return jnp.broadcast_to(m[:, li:li + 1], (SSD_CHUNK, SSD_CHUNK))

    def pair_cols(m, li):
        return jnp.where(lo, bcol(m, li), bcol(m, li + 1))

    def chunk_step(c, d):
        r0 = pl.multiple_of(c * SSD_CHUNK, SSD_CHUNK)
        rows = pl.ds(r0, SSD_CHUNK)
        mask = (ri >= ci) if d == 0 else (ri <= ci)
        da_c = da_s[rows, :]
        acum = jnp.dot(mask.astype(F32), da_c, preferred_element_type=F32, precision=lax.Precision.HIGHEST)
        acum_t = acum.T
        tot = acum[SSD_CHUNK - 1:SSD_CHUNK, :] if d == 0 else acum[0:1, :]
        eac = jnp.exp(acum)
        wdec = jnp.exp(tot - acum)
        etot = jnp.exp(tot)
        dt_c = dt_s[rows, :]
        xs = xc_s[rows, 0:SSD_WIDTH]
        y_parts = []
        for g in range(SSD_GROUPS):
            b_g = xc_s[rows, SSD_WIDTH + g * SSD_STATE:SSD_WIDTH + (g + 1) * SSD_STATE].astype(BF16)
            c0 = SSD_WIDTH + SSD_GROUPS * SSD_STATE + g * SSD_STATE
            c_g = xc_s[rows, c0:c0 + SSD_STATE].astype(BF16)
            cb = _dot_nt(c_g, b_g)
            s_g = s_s[g]
            yoff = _dot_nt(c_g, s_g.astype(BF16))
            xw_parts = []
            for pp in range(2):
                h0 = 4 * g + 2 * pp
                li = SSD_HEADS * d + h0
                sc = []
                for e in range(2):
                    col = bcol(acum, li + e)
                    rowb = jnp.broadcast_to(acum_t[li + e:li + e + 1, :], (SSD_CHUNK, SSD_CHUNK))
                    lm = jnp.exp(jnp.where(mask, col - rowb, NEG_INF))
                    sc.append((cb * lm).astype(BF16))
                scp = jnp.concatenate(sc, axis=1)
                xdt = xs[:, LANE * (2 * g + pp):LANE * (2 * g + pp + 1)] * pair_cols(dt_c, li)
                bd = jnp.concatenate([jnp.where(lo, xdt, 0.0), jnp.where(lo, 0.0, xdt)], axis=0).astype(BF16)
                ydiag = _dot(scp, bd)
                y_parts.append(ydiag + yoff[:, LANE * pp:LANE * (pp + 1)] * pair_cols(eac, li))
                xw_parts.append(xdt * pair_cols(wdec, li))
            xw_t = jnp.concatenate(xw_parts, axis=1).T.astype(BF16)
            ds = _dot(xw_t, b_g)
            dec = jnp.concatenate(
                [jnp.broadcast_to(etot[:, SSD_HEADS * d + 4 * g + hh:SSD_HEADS * d + 4 * g + hh + 1],
                                  (SSD_HEAD_DIM, SSD_STATE)) for hh in range(4)], axis=0)
            s_s[g] = s_g * dec + ds
        y_c = jnp.concatenate(y_parts, axis=1)
        arow = pl.ds(pl.multiple_of(blk * tb + r0, SSD_CHUNK), SSD_CHUNK)
        if d == 0:
            yf_s[arow, :] = y_c + xs * dsk_ref[...]
        else:
            y = yf_s[arow, :] + y_c
            gated = y * _silu(z_ref[rows, :].astype(F32))
            y_ref[rows, :] = _rms(gated, gn_ref[...]).astype(BF16)

    @pl.when(j < nblk)
    def _():
        def body(c, carry):
            chunk_step(c, 0)
            return carry
        lax.fori_loop(0, nch, body, 0)

    @pl.when(j >= nblk)
    def _():
        def body(c, carry):
            chunk_step(nch - 1 - c, 1)
            return carry
        lax.fori_loop(0, nch, body, 0)

    if want_state:
        @pl.when(j == 2 * nblk - 1)
        def _():
            st_ref[1] = s_s[...]


def _ssd(xbc, misc, z, lw, n, seq_len, init, want_state):
    tb = min(512, seq_len)
    nblk = seq_len // tb
    hpb = tb // _HALO
    has_init = init is not None

    def blk_of(j):
        return jnp.where(j < nblk, j, 2 * nblk - 1 - j)

    main = lambda b, j: (b * nblk + blk_of(j), 0)
    prev = lambda b, j: (jnp.maximum((b * nblk + blk_of(j)) * hpb - 1, 0), 0)
    nxt = lambda b, j: (jnp.minimum((b * nblk + blk_of(j) + 1) * hpb, n * nblk * hpb - 1), 0)
    const = lambda b, j: (0, 0)
    in_specs = [pl.BlockSpec((tb, SSD_CONV_CH), main),
                pl.BlockSpec((_HALO, SSD_CONV_CH), prev),
                pl.BlockSpec((_HALO, SSD_CONV_CH), nxt),
                pl.BlockSpec((tb, LANE), main),
                pl.BlockSpec((tb, SSD_WIDTH), main),
                pl.BlockSpec((3, SSD_CONV_CH), const), pl.BlockSpec((1, SSD_CONV_CH), const),
                pl.BlockSpec((1, LANE), const), pl.BlockSpec((1, LANE), const),
                pl.BlockSpec((1, SSD_WIDTH), const), pl.BlockSpec((1, SSD_WIDTH), const)]
    args = [xbc, xbc, xbc, misc, z, lw["conv_w"], lw["conv_b"], lw["dt_bias"], lw["a_log"], lw["d_skip"],
            lw["ssd_norm_g"]]
    st_block = (None, 2, SSD_GROUPS, 4 * SSD_HEAD_DIM, SSD_STATE)
    if has_init:
        in_specs.append(pl.BlockSpec(st_block, lambda b, j: (b, 0, 0, 0, 0)))
        args.append(init)
    omap = lambda b, j: (b * nblk + jnp.where(j < nblk, nblk - 1, 2 * nblk - 1 - j), 0)
    out_specs = [pl.BlockSpec((tb, SSD_WIDTH), omap)]
    out_shape = [jax.ShapeDtypeStruct((n * seq_len, SSD_WIDTH), BF16)]
    if want_state:
        out_specs.append(pl.BlockSpec(st_block, lambda b, j: (b, 0, 0, 0, 0)))
        out_shape.append(jax.ShapeDtypeStruct((n,) + st_block[1:], F32))
    res = pl.pallas_call(
        functools.partial(_ssd_kernel, nblk, tb, has_init, want_state),
        grid=(n, 2 * nblk),
        in_specs=in_specs,
        out_specs=out_specs,
        out_shape=out_shape,
        scratch_shapes=[pltpu.VMEM((seq_len, SSD_WIDTH), F32),
                        pltpu.VMEM((SSD_GROUPS, 4 * SSD_HEAD_DIM, SSD_STATE), F32),
                        pltpu.VMEM((tb, SSD_CONV_CH), F32),
                        pltpu.VMEM((tb, LANE), F32),
                        pltpu.VMEM((tb, LANE), F32)],
        compiler_params=_cparams(("arbitrary", "arbitrary")),
        name="ssd_lat" if has_init else "ssd_ctx",
    )(*args)
    return res if want_state else (res[0], None)


def _attn_kernel(kind, lam_init, *refs):
    if kind == "mla":
        q_ref, k_ref, v_ref, o_ref, m_ref, acc_ref, s_scr, p_scr, a_scr = refs
    else:
        q_ref, k_ref, v_ref, lamv_ref, g_ref, o_ref, m_ref, acc_ref, s_scr, p_scr, a_scr, l_ref = refs
    ki = pl.program_id(2)
    nk = pl.num_programs(2)
    tq, tk = s_scr.shape[1], s_scr.shape[2]

    @pl.when(ki == 0)
    def _():
        m_ref[...] = jnp.full(m_ref.shape, NEG_INF, F32)
        acc_ref[...] = jnp.zeros(acc_ref.shape, F32)
        if kind != "mla":
            l_ref[...] = jnp.zeros(l_ref.shape, F32)

    def kv_lanes(vh):
        kv = vh if kind == "mla" else vh // 2
        return slice(kv * LANE, (kv + 1) * LANE)

    def scores(vh):
        s_scr[vh] = _dot_nt(q_ref[:, vh * LANE:(vh + 1) * LANE], k_ref[:, kv_lanes(vh)])

    scores(0)
    for vh in range(ATTN_HEADS):
        ksl = kv_lanes(vh)
        if vh + 1 < ATTN_HEADS:
            scores(vh + 1)
        for i in range(tq // ATTN_STRIP):
            r = slice(i * ATTN_STRIP, (i + 1) * ATTN_STRIP)
            s = s_scr[vh, r, :]
            m_prev = m_ref[vh, r, :]
            m_new = jnp.maximum(m_prev, jnp.max(s, axis=-1, keepdims=True))
            alpha = jnp.exp2(m_prev - m_new)
            p = jnp.exp2(s - jnp.concatenate([m_new] * (tk // LANE), axis=1))
            if kind != "mla":
                l_ref[vh, r, :] = alpha * l_ref[vh, r, :] + jnp.sum(p, axis=-1, keepdims=True)
            m_ref[vh, r, :] = m_new
            a_scr[vh, r, :] = alpha
            p_scr[vh, r, :] = p.astype(BF16)
        acc_ref[vh] = acc_ref[vh] * a_scr[vh] + _dot(p_scr[vh], v_ref[:, ksl])

    @pl.when(ki == nk - 1)
    def _():
        if kind == "mla":
            lo = _lane_iota((tq, LANE)) < MLA_V

            def head_out(hd):
                a = acc_ref[hd]
                return a / jnp.broadcast_to(a[:, MLA_V:MLA_V + 1], a.shape)

            for jp in range(MLA_HEADS // 2):
                pair = jnp.where(lo, head_out(2 * jp), pltpu.roll(head_out(2 * jp + 1), MLA_V, axis=1))
                o_ref[:, jp * LANE:(jp + 1) * LANE] = pair.astype(BF16)
        else:
            lv = lamv_ref[...]
            lam = (jnp.exp(jnp.sum(lv[0:1, :] * lv[1:2, :], axis=-1, keepdims=True))
                   - jnp.exp(jnp.sum(lv[2:3, :] * lv[3:4, :], axis=-1, keepdims=True)) + lam_init)
            for hd in range(DIFF_HEADS):
                o = acc_ref[2 * hd] / l_ref[2 * hd] - lam * (acc_ref[2 * hd + 1] / l_ref[2 * hd + 1])
                o_ref[:, hd * LANE:(hd + 1) * LANE] = (_rms(o, g_ref[...]) * (1.0 - lam_init)).astype(BF16)


def _attention(kind, q, k, v, n, seq_len, key_len, extra=None, lam_init=None):
    tq = min(512, seq_len)
    tk = min(512, seq_len)
    nq, nk = seq_len // tq, key_len // tk
    wq, wk, wv = q.shape[1], k.shape[1], v.shape[1]
    qmap = lambda b, qi, ki: (b * nq + qi, 0)
    kmap = lambda b, qi, ki: (b * nk + ki, 0)
    in_specs = [pl.BlockSpec((tq, wq), qmap), pl.BlockSpec((tk, wk), kmap), pl.BlockSpec((tk, wv), kmap)]
    args = [q, k, v]
    stat = pltpu.VMEM((ATTN_HEADS, tq, LANE), F32)
    scratch = [stat, stat, pltpu.VMEM((ATTN_HEADS, tq, tk), F32), pltpu.VMEM((ATTN_HEADS, tq, tk), BF16), stat]
    if kind == "mla":
        w_out = MLA_HEADS * MLA_V
    else:
        w_out = DIFF_WIDTH
        scratch.append(stat)
        lamv, g = extra
        in_specs += [pl.BlockSpec(lamv.shape, lambda b, qi, ki: (0, 0)),
                     pl.BlockSpec(g.shape, lambda b, qi, ki: (0, 0))]
        args += [lamv, g]
    return pl.pallas_call(
        functools.partial(_attn_kernel, kind, lam_init),
        grid=(n, nq, nk),
        in_specs=in_specs,
        out_specs=pl.BlockSpec((tq, w_out), qmap),
        out_shape=jax.ShapeDtypeStruct((n * seq_len, w_out), BF16),
        scratch_shapes=scratch,
        compiler_params=_cparams(("arbitrary", "arbitrary", "arbitrary")),
        name=kind + ("_lat" if key_len > seq_len else "_ctx"),
    )(*args)


def _outproj_kernel(ys_ref, om_ref, od_ref, x_ref, w_ref, g1_ref, n2_ref, sc_ref, sh_ref, wr_ref,
                    x1_ref, h2_ref, sco_ref):
    mix = (_dot(ys_ref[...], w_ref[0:512, :]) + _dot(om_ref[...], w_ref[512:1024, :])
           + _dot(od_ref[...], w_ref[1024:1536, :]))
    x1 = x_ref[...] + g1_ref[...] * mix
    x1_ref[...] = x1
    h2 = _rms(x1, n2_ref[...]) * (1.0 + sc_ref[...]) + sh_ref[...]
    h2_ref[...] = h2
    sco_ref[...] = 1.0 / (1.0 + jnp.exp(-_dot(h2.astype(BF16), wr_ref[...])))


def _outproj(ys, om, od, x2d, g1, sc2, sh2, lw, seq_len):
    t_tot, d = x2d.shape
    tm = min(512, seq_len)
    tpb = seq_len // tm
    nb = g1.shape[0]
    bidx = (lambda i: (i // tpb, 0, 0)) if nb > 1 else (lambda i: (0, 0, 0))
    row = lambda i: (i, 0)
    const = lambda i: (0, 0)
    mod = pl.BlockSpec((None, 1, d), bidx)
    return pl.pallas_call(
        _outproj_kernel,
        grid=(t_tot // tm,),
        in_specs=[pl.BlockSpec((tm, 512), row)] * 3 + [
            pl.BlockSpec((tm, d), row), pl.BlockSpec((1536, d), const), mod,
            pl.BlockSpec((1, d), const), mod, mod, pl.BlockSpec((d, LANE), const)],
        out_specs=[pl.BlockSpec((tm, d), row), pl.BlockSpec((tm, d), row), pl.BlockSpec((tm, LANE), row)],
        out_shape=[jax.ShapeDtypeStruct((t_tot, d), F32), jax.ShapeDtypeStruct((t_tot, d), F32),
                   jax.ShapeDtypeStruct((t_tot, LANE), F32)],
        compiler_params=_cparams(("arbitrary",)),
        name="outproj",
    )(ys, om, od, x2d, lw["w_out"], g1, lw["norm2_g"], sc2, sh2, lw["w_router"])


def _route_kernel(sco_ref, bias_ref, idx_ref, w_ref, rank_ref, cnt_ref, run_s):
    i = pl.program_id(0)
    tm = sco_ref.shape[0]
    gsz = N_EXPERTS // N_EXPERT_GROUPS

    @pl.when(i == 0)
    def _():
        run_s[...] = jnp.zeros(run_s.shape, F32)

    shape3 = (N_EXPERT_GROUPS, gsz, tm)
    sco = sco_ref[...]
    scores3 = sco.T[0:N_EXPERTS, :].reshape(shape3)
    sel3 = (sco + bias_ref[...]).T[0:N_EXPERTS, :].reshape(shape3)
    sub = lax.broadcasted_iota(jnp.int32, shape3, 1)
    gid = lax.broadcasted_iota(jnp.int32, shape3, 0)
    eid = gid * gsz + sub

    def max_all(a):
        return jnp.max(jnp.max(a, axis=0, keepdims=True), axis=1, keepdims=True)

    def min_all(a):
        return jnp.min(jnp.min(a, axis=0, keepdims=True), axis=1, keepdims=True)

    def sum_all(a):
        return jnp.sum(jnp.sum(a, axis=0, keepdims=True), axis=1, keepdims=True)

    m1 = jnp.max(sel3, axis=1, keepdims=True)
    first = jnp.min(jnp.where(sel3 == m1, sub, gsz), axis=1, keepdims=True)
    m2 = jnp.max(jnp.where(sub == first, NEG_INF, sel3), axis=1, keepdims=True)
    gscore = m1 + m2
    gid1 = lax.broadcasted_iota(jnp.int32, gscore.shape, 0)
    gsel = jnp.zeros(gscore.shape, F32)
    for _ in range(TOPK_GROUPS):
        gm = jnp.max(gscore, axis=0, keepdims=True)
        gfirst = jnp.min(jnp.where(gscore == gm, gid1, N_EXPERT_GROUPS), axis=0, keepdims=True)
        pick = gid1 == gfirst
        gsel = jnp.where(pick, 1.0, gsel)
        gscore = jnp.where(pick, NEG_INF, gscore)
    selm = jnp.where(gsel > 0.5, sel3, NEG_INF)
    chosen = jnp.zeros(shape3, F32)
    idxs, ws = [], []
    for _ in range(TOP_K):
        mx = max_all(selm)
        efirst = min_all(jnp.where(selm == mx, eid, N_EXPERTS))
        pick = eid == efirst
        chosen = jnp.where(pick, 1.0, chosen)
        idxs.append(efirst)
        ws.append(sum_all(jnp.where(pick, scores3, 0.0)))
        selm = jnp.where(pick, NEG_INF, selm)
    wsum = ws[0]
    for k in range(1, TOP_K):
        wsum = wsum + ws[k]
    r = lax.broadcasted_iota(jnp.int32, (tm, tm), 0)
    c = lax.broadcasted_iota(jnp.int32, (tm, tm), 1)
    before = jnp.where(r < c, 1.0, 0.0).astype(BF16)
    cm = chosen.reshape(N_EXPERTS, tm)
    run = run_s[...]
    excl3 = (_dot(cm.astype(BF16), before) + run[:, 0:1]).reshape(shape3)
    run_s[...] = run + jnp.sum(cm, axis=1, keepdims=True)
    for k in range(TOP_K):
        idx_ref[k:k + 1, :] = idxs[k].reshape(1, tm)
        w_ref[k:k + 1, :] = (ws[k] / wsum * ROUTED_SCALE).reshape(1, tm)
        rk = sum_all(jnp.where(eid == idxs[k], excl3, 0.0))
        rank_ref[k:k + 1, :] = rk.reshape(1, tm).astype(jnp.int32)
    cnt_ref[...] = run_s[...]


def _route(scores, bias_row):
    t_tot = scores.shape[0]
    tm = 512
    col = lambda i: (0, i)
    return pl.pallas_call(
        _route_kernel,
        grid=(t_tot // tm,),
        in_specs=[pl.BlockSpec((tm, LANE), lambda i: (i, 0)), pl.BlockSpec((1, LANE), lambda i: (0, 0))],
        out_specs=[pl.BlockSpec((TOP_K, tm), col), pl.BlockSpec((TOP_K, tm), col),
                   pl.BlockSpec((TOP_K, tm), col), pl.BlockSpec((N_EXPERTS, LANE), lambda i: (0, 0))],
        out_shape=[jax.ShapeDtypeStruct((TOP_K, t_tot), jnp.int32), jax.ShapeDtypeStruct((TOP_K, t_tot), F32),
                   jax.ShapeDtypeStruct((TOP_K, t_tot), jnp.int32),
                   jax.ShapeDtypeStruct((N_EXPERTS, LANE), F32)],
        scratch_shapes=[pltpu.VMEM((N_EXPERTS, LANE), F32)],
        compiler_params=_cparams(("arbitrary",)),
        name="route",
    )(scores, bias_row)


def _slots_kernel(ps_ref, idx_ref, rank_ref, dest_ref):
    idx = idx_ref[...]
    dest = rank_ref[...]
    for e in range(N_EXPERTS):
        dest = dest + jnp.where(idx == e, ps_ref[e], 0)
    dest_ref[...] = dest


def _slots(pad_start, idx_t, rank_t):
    t_tot = idx_t.shape[1]
    tm = min(2048, t_tot)
    blk = lambda: pl.BlockSpec((TOP_K, tm), lambda i, ps: (0, i))
    return pl.pallas_call(
        _slots_kernel,
        grid_spec=pltpu.PrefetchScalarGridSpec(
            num_scalar_prefetch=1, grid=(t_tot // tm,), in_specs=[blk(), blk()], out_specs=blk()),
        out_shape=jax.ShapeDtypeStruct((TOP_K, t_tot), jnp.int32),
        compiler_params=_cparams(("arbitrary",)),
        name="slots",
    )(pad_start, idx_t, rank_t)


def _dispatch_kernel(h_ref, dest_ref, _, xs_ref, sem):
    tm = h_ref.shape[0]

    def body(t, carry):
        for k in range(TOP_K):
            dst = xs_ref.at[pl.ds(dest_ref[k, t], 1), :]
            pltpu.make_async_copy(h_ref.at[pl.ds(t, 1), :], dst, sem).start()
        return carry

    lax.fori_loop(0, tm, body, 0, unroll=DMA_UNROLL)
    for k in range(TOP_K):
        pltpu.make_async_copy(h_ref, xs_ref.at[pl.ds(0, tm), :], sem).wait()


def _dispatch(h2, dest, slots):
    t_tot, d = h2.shape
    tm = 256
    xs0 = jnp.zeros((slots, d), F32)
    return pl.pallas_call(
        _dispatch_kernel,
        grid=(t_tot // tm,),
        in_specs=[pl.BlockSpec((tm, d), lambda i: (i, 0)),
                  pl.BlockSpec((TOP_K, tm), lambda i: (0, i), memory_space=pltpu.SMEM),
                  pl.BlockSpec(memory_space=pl.ANY)],
        out_specs=pl.BlockSpec(memory_space=pl.ANY),
        out_shape=jax.ShapeDtypeStruct((slots, d), F32),
        scratch_shapes=[pltpu.SemaphoreType.DMA(())],
        input_output_aliases={2: 0},
        compiler_params=_cparams(("arbitrary",)),
        name="dispatch",
    )(h2, dest, xs0)


def _experts_kernel(bexp_ref, nused_ref, x_ref, wg_ref, wu_ref, wd_ref, y_ref, wg_b, wu_b, wd_b):
    i = pl.program_id(0)
    used = i < nused_ref[0]
    fresh = (i == 0) | (bexp_ref[i] != bexp_ref[jnp.maximum(i - 1, 0)])

    @pl.when(used & fresh)
    def _():
        wg_b[...] = wg_ref[...].astype(BF16)
        wu_b[...] = wu_ref[...].astype(BF16)
        wd_b[...] = wd_ref[...].astype(BF16)

    @pl.when(used)
    def _():
        x = x_ref[...].astype(BF16)
        hmid = _silu(_dot(x, wg_b[...])) * _dot(x, wu_b[...])
        y_ref[...] = _dot(hmid.astype(BF16), wd_b[...])

    @pl.when(jnp.logical_not(used))
    def _():
        y_ref[...] = jnp.zeros(y_ref.shape, F32)


def _experts(x_sorted, block_exp, n_used, wg, wu, wd, layer):
    slots, d = x_sorted.shape
    n_blocks = slots // EXPERT_ROWS
    hid = wg.shape[-1]
    wmap = lambda i, be, nu: (layer, be[i], 0, 0)
    return pl.pallas_call(
        _experts_kernel,
        grid_spec=pltpu.PrefetchScalarGridSpec(
            num_scalar_prefetch=2,
            grid=(n_blocks,),
            in_specs=[pl.BlockSpec((EXPERT_ROWS, d), lambda i, be, nu: (i, 0)),
                      pl.BlockSpec((None, None, d, hid), wmap),
                      pl.BlockSpec((None, None, d, hid), wmap),
                      pl.BlockSpec((None, None, hid, d), wmap)],
            out_specs=pl.BlockSpec((EXPERT_ROWS, d), lambda i, be, nu: (i, 0)),
            scratch_shapes=[pltpu.VMEM((d, hid), BF16), pltpu.VMEM((d, hid), BF16), pltpu.VMEM((hid, d), BF16)]),
        out_shape=jax.ShapeDtypeStruct((slots, d), F32),
        compiler_params=_cparams(("arbitrary",)),
        name="experts",
    )(block_exp, n_used, x_sorted, wg, wu, wd)


def _combine_kernel(final, h2_ref, dest_ref, w_ref, x1_ref, g2_ref, wsg_ref, wsu_ref, wsd_ref,
                    fg_ref, ys_ref, o_ref, yg_s, sem):
    tm = h2_ref.shape[0]

    def body(t, carry):
        for k in range(TOP_K):
            src = ys_ref.at[pl.ds(dest_ref[k, t], 1), :]
            pltpu.make_async_copy(src, yg_s.at[k, pl.ds(t, 1), :], sem).start()
        return carry

    lax.fori_loop(0, tm, body, 0, unroll=DMA_UNROLL)
    h2 = h2_ref[...].astype(BF16)
    shared = _dot((_silu(_dot(h2, wsg_ref[...])) * _dot(h2, wsu_ref[...])).astype(BF16), wsd_ref[...])
    for k in range(TOP_K):
        pltpu.make_async_copy(ys_ref.at[pl.ds(0, tm), :], yg_s.at[k], sem).wait()
    w = w_ref[...]
    routed = yg_s[0] * w[:, 0:1]
    for k in range(1, TOP_K):
        routed = routed + yg_s[k] * w[:, k:k + 1]
    x2 = x1_ref[...] + g2_ref[...] * (routed + shared)
    o_ref[...] = _rms(x2, fg_ref[...]) if final else x2


def _combine(h2, dest, w_tok, y_sorted, x1, g2, lw, final_g, seq_len, final):
    t_tot, d = x1.shape
    tm = min(256, seq_len)
    tpb = seq_len // tm
    nb = g2.shape[0]
    bidx = (lambda i: (i // tpb, 0, 0)) if nb > 1 else (lambda i: (0, 0, 0))
    row = lambda i: (i, 0)
    const = lambda i: (0, 0)
    hid = lw["w_s_gate"].shape[1]
    return pl.pallas_call(
        functools.partial(_combine_kernel, final),
        grid=(t_tot // tm,),
        in_specs=[pl.BlockSpec((tm, d), row),
                  pl.BlockSpec((TOP_K, tm), lambda i: (0, i), memory_space=pltpu.SMEM),
                  pl.BlockSpec((tm, TOP_K), row),
                  pl.BlockSpec((tm, d), row), pl.BlockSpec((None, 1, d), bidx),
                  pl.BlockSpec((d, hid), const), pl.BlockSpec((d, hid), const), pl.BlockSpec((hid, d), const),
                  pl.BlockSpec((1, d), const), pl.BlockSpec(memory_space=pl.ANY)],
        out_specs=pl.BlockSpec((tm, d), row),
        out_shape=jax.ShapeDtypeStruct((t_tot, d), F32),
        scratch_shapes=[pltpu.VMEM((TOP_K, tm, d), F32), pltpu.SemaphoreType.DMA(())],
        compiler_params=_cparams(("arbitrary",)),
        name="combine_final" if final else "combine",
    )(h2, dest, w_tok, x1, g2, lw["w_s_gate"], lw["w_s_up"], lw["w_s_down"], final_g, y_sorted)


def _rope_tables(seq_len):
    t = jnp.arange(seq_len, dtype=jnp.int32)
    rows = (t // GRID_W).astype(F32)[:, None]
    cols = (t % GRID_W).astype(F32)[:, None]
    lane = jnp.arange(LANE)

    def build(rel, width, active):
        dr = width // 2
        half = dr // 2
        within = rel % dr
        freq = ROPE_BASE ** (-(within % half).astype(F32) / half)
        pos = jnp.where((rel < dr)[None, :], rows, cols)
        ang = pos * freq[None, :]
        sign = jnp.where(within < half, -1.0, 1.0)
        cos = jnp.where(active[None, :], jnp.cos(ang), 1.0)
        sin = jnp.where(active[None, :], jnp.sin(ang) * sign[None, :], 0.0)
        return cos.astype(F32), sin.astype(F32)

    act_q = (lane >= _KROPE_LANE) & (lane < _KROPE_LANE + MLA_ROPE_DIM)
    cq, sq = build(jnp.clip(lane - _KROPE_LANE, 0, MLA_ROPE_DIM - 1), MLA_ROPE_DIM, act_q)
    cd, sd = build(lane % DIFF_HEAD_DIM, DIFF_HEAD_DIM, jnp.ones((LANE,), bool))
    return cq, sq, cd, sd


def _prep_layer(p, l):
    w_in = p["w_in"][l]
    d = w_in.shape[0]
    z0, x0, t0, q0, kv0, dq0, dk0, dv0 = 0, 512, 1536, 1552, 1808, 1968, 2480, 2992
    misc = jnp.zeros((d, LANE), F32)
    misc = misc.at[:, 0:2 * SSD_HEADS].set(w_in[:, t0:q0])
    misc = misc.at[:, _KROPE_LANE:_KROPE_LANE + MLA_ROPE_DIM].set(w_in[:, kv0 + MLA_KV_RANK:dq0])
    w_cat = jnp.concatenate([w_in[:, z0:x0], w_in[:, x0:t0], misc, w_in[:, q0:kv0],
                             w_in[:, kv0:kv0 + MLA_KV_RANK], w_in[:, dq0:dk0], w_in[:, dk0:dv0],
                             w_in[:, dv0:]], axis=1).astype(BF16)
    qd = MLA_NOPE + MLA_ROPE_DIM
    w_qb = jnp.pad(p["w_q_b"][l].reshape(MLA_Q_RANK, MLA_HEADS, qd), ((0, 0), (0, 0), (0, LANE - qd)))
    w_kvb = p["w_kv_b"][l].reshape(MLA_KV_RANK, MLA_HEADS, MLA_NOPE + MLA_V)
    w_k = jnp.pad(w_kvb[:, :, :MLA_NOPE], ((0, 0), (0, 0), (0, LANE - MLA_NOPE)))
    vv = w_kvb[:, :, MLA_NOPE:]
    w_v = jnp.pad(vv, ((0, 0), (0, 0), (0, LANE - MLA_V)))
    pad_lane = lambda a: jnp.pad(a, ((0, 0), (0, LANE - a.shape[1])))
    lamv = jnp.stack([p["lambda_q1"][l], p["lambda_k1"][l], p["lambda_q2"][l], p["lambda_k2"][l]])
    return {
        "w_in": w_cat,
        "w_qb": w_qb.reshape(MLA_Q_RANK, MLA_HEADS * LANE).astype(BF16),
        "w_k": w_k.reshape(MLA_KV_RANK, MLA_HEADS * LANE).astype(BF16),
        "w_v": w_v.reshape(MLA_KV_RANK, MLA_HEADS * LANE).astype(BF16),
        "gqa": p["q_a_norm_g"][l][None, :], "gkv": p["kv_a_norm_g"][l][None, :],
        "norm1_g": p["norm1_g"][l][None, :], "norm2_g": p["norm2_g"][l][None, :],
        "conv_w": p["conv_w"][l], "conv_b": p["conv_b"][l][None, :],
        "dt_bias": pad_lane(p["dt_bias"][l].reshape(1, -1)), "a_log": pad_lane(p["a_log"][l].reshape(1, -1)),
        "d_skip": jnp.repeat(p["d_skip"][l], SSD_HEAD_DIM)[None, :], "ssd_norm_g": p["ssd_norm_g"][l][None, :],
        "lamv": pad_lane(lamv), "subln_g": p["diff_subln_g"][l][None, :],
        "w_out": p["w_out"][l].astype(BF16),
        "w_router": pad_lane(p["w_router"][l]).astype(BF16),
        "router_bias": pad_lane(p["router_bias"][l][None, :]),
        "w_s_gate": p["w_s_gate"][l].astype(BF16), "w_s_up": p["w_s_up"][l].astype(BF16),
        "w_s_down": p["w_s_down"][l].astype(BF16),
    }


def _lambda_init(layer_idx):
    return 0.8 - 0.6 * math.exp(-0.3 * layer_idx)


def _moe(h2, scores, x1, g2, lw, ew, layer, final_g, seq_len, final):
    t_tot, d = h2.shape
    idx_t, w_t, rank_t, cnt = _route(scores, lw["router_bias"])
    counts = cnt[:, 0].astype(jnp.int32)
    padded = (counts + EXPERT_ROWS - 1) // EXPERT_ROWS * EXPERT_ROWS
    pad_end = jnp.cumsum(padded)
    pad_start = pad_end - padded
    n_blocks = t_tot * TOP_K // EXPERT_ROWS + N_EXPERTS
    block_start = jnp.arange(n_blocks, dtype=jnp.int32) * EXPERT_ROWS
    block_exp = jnp.minimum(jnp.sum((pad_end[None, :] <= block_start[:, None]).astype(jnp.int32), axis=1),
                            N_EXPERTS - 1)
    n_used = pad_end[-1:] // EXPERT_ROWS
    dest = _slots(pad_start, idx_t, rank_t)
    x_sorted = _dispatch(h2, dest, n_blocks * EXPERT_ROWS)
    y_sorted = _experts(x_sorted, block_exp, n_used, ew[0], ew[1], ew[2], layer)
    return _combine(h2, dest, w_t.T, y_sorted, x1, g2, lw, final_g, seq_len, final)


def _block(x2d, mod, lw, ew, layer_idx, n, seq_len, tables, cached, final_g, final):
    sh1, sc1, g1, sh2, sc2, g2 = [mod[:, i:i + 1, :] for i in range(6)]
    latent = cached is not None
    lam_init = _lambda_init(layer_idx)
    if latent:
        past = cached["ckv"].shape[1]
        kr_pad = jnp.pad(cached["krope"].reshape(n * past, MLA_ROPE_DIM),
                         ((0, 0), (_KROPE_LANE, LANE - _KROPE_LANE - MLA_ROPE_DIM)))
        kv_bufs = _ctxkv(cached["ckv"].reshape(n * past, MLA_KV_RANK), kr_pad,
                         cached["dk"].reshape(n * past, DIFF_WIDTH), cached["dv"].reshape(n * past, DIFF_WIDTH),
                         lw, n, seq_len, past)
        res = _inproj(x2d, sc1, sh1, lw["norm1_g"], lw, seq_len, tables, kv_bufs, past)
        init, want_state, key_len = cached["ssd"], False, seq_len + past
    else:
        res = _inproj(x2d, sc1, sh1, lw["norm1_g"], lw, seq_len, None, None, 0)
        init, want_state, key_len = None, True, seq_len
    z, xbc, misc, q, dq, k, v, dk, dv = res[:9]
    y_ssd, st = _ssd(xbc, misc, z, lw, n, seq_len, init, want_state)
    o_mla = _attention("mla", q, k, v, n, seq_len, key_len)
    o_diff = _attention("diff", dq, dk, dv, n, seq_len, key_len, (lw["lamv"], lw["subln_g"]), lam_init)
    ctx_new = None
    if not latent:
        ckv_f, dk_f, dv_f = res[9:]
        ctx_new = (st.reshape(n, 2, SSD_HEADS, SSD_HEAD_DIM, SSD_STATE),
                   ckv_f.reshape(n, seq_len, MLA_KV_RANK),
                   misc[:, _KROPE_LANE:_KROPE_LANE + MLA_ROPE_DIM].reshape(n, seq_len, MLA_ROPE_DIM),
                   dk_f.reshape(n, seq_len, DIFF_HEADS, 2 * DIFF_HEAD_DIM),
                   dv_f.reshape(n, seq_len, DIFF_HEADS, 2 * DIFF_HEAD_DIM))
    x1, h2, scores = _outproj(y_ssd, o_mla, o_diff, x2d, g1, sc2, sh2, lw, seq_len)
    x2 = _moe(h2, scores, x1, g2, lw, ew, layer_idx, final_g, seq_len, final)
    return x2, ctx_new


def kernel(x_prompt, x_sample, state_ssd, cache_mla_ckv, cache_mla_krope, cache_diff_k, cache_diff_v, c, c_ctx, w_ada, b_ada, norm1_g, norm2_g, w_in, conv_w, conv_b, dt_bias, a_log, d_skip, ssd_norm_g, q_a_norm_g, w_q_b, kv_a_norm_g, w_kv_b, lambda_q1, lambda_k1, lambda_q2, lambda_k2, diff_subln_g, w_out, w_router, router_bias, w_e_gate, w_e_up, w_e_down, w_s_gate, w_s_up, w_s_down, final_norm_g):
    p = dict(norm1_g=norm1_g, norm2_g=norm2_g, w_in=w_in, conv_w=conv_w, conv_b=conv_b, dt_bias=dt_bias,
             a_log=a_log, d_skip=d_skip, ssd_norm_g=ssd_norm_g, q_a_norm_g=q_a_norm_g, w_q_b=w_q_b,
             kv_a_norm_g=kv_a_norm_g, w_kv_b=w_kv_b, lambda_q1=lambda_q1, lambda_k1=lambda_k1,
             lambda_q2=lambda_q2, lambda_k2=lambda_k2, diff_subln_g=diff_subln_g, w_out=w_out,
             w_router=w_router, router_bias=router_bias, w_s_gate=w_s_gate, w_s_up=w_s_up, w_s_down=w_s_down)
    ew = (w_e_gate, w_e_up, w_e_down)
    depth = w_in.shape[0]
    nc, lc, d = x_prompt.shape
    ns, ls, _ = x_sample.shape
    cond_rows = 16
    cond = jnp.zeros((cond_rows, d), F32).at[0:ns].set(c).at[ns].set(c_ctx)
    mod = _ada(cond, w_ada, b_ada).reshape(depth, cond_rows, 6, d)
    tables = _rope_tables(ls)
    final_g = final_norm_g[None, :]
    xp = x_prompt.reshape(nc * lc, d)
    xs = x_sample.reshape(ns * ls, d)
    news = []
    for l in range(depth):
        lw = _prep_layer(p, l)
        final = l == depth - 1
        xp, ctx_new = _block(xp, mod[l, ns:ns + 1], lw, ew, l, nc, lc, None, None, final_g, final)
        news.append(ctx_new)
        cached = {"ssd": state_ssd[:, l].reshape(ns, 2, SSD_GROUPS, 4 * SSD_HEAD_DIM, SSD_STATE),
                  "ckv": cache_mla_ckv[:, l], "krope": cache_mla_krope[:, l],
                  "dk": cache_diff_k[:, l], "dv": cache_diff_v[:, l]}
        xs, _ = _block(xs, mod[l, 0:ns], lw, ew, l, ns, ls, tables, cached, final_g, final)
    stack = lambda i: jnp.stack([nw[i] for nw in news], axis=1)
    return (xp.reshape(nc, lc, d), xs.reshape(ns, ls, d), stack(0), stack(1), stack(2), stack(3), stack(4))
```
